```python
import math
import jax
import jax.numpy as jnp
from jax import lax
import numpy as np

D_MODEL = 1024
BATCH = 8
SEQ = 4096
DEPTH = 4

CTX_LEN = 256
GRID_W = 64
N_MIXERS = 2
N_SSD_LAYERS = (DEPTH + 1) // 2
N_RET_LAYERS = DEPTH // 2

DEEPNORM_ALPHA = (2.0 * DEPTH) ** 0.25
DEEPNORM_BETA = (8.0 * DEPTH) ** -0.25
LN_EPS = 1e-5

SSD_D_INNER = 2 * D_MODEL
SSD_HEADDIM = 64
SSD_HEADS = SSD_D_INNER // SSD_HEADDIM
SSD_GROUPS = 4
SSD_HPG = SSD_HEADS // SSD_GROUPS
SSD_STATE = 128
SSD_CONV_W = 5
SSD_CHUNK = 128
SSD_BC_DIM = SSD_GROUPS * SSD_STATE
SSD_CONV_DIM = SSD_D_INNER + 2 * SSD_BC_DIM
SSD_IN_DIM = SSD_D_INNER + SSD_CONV_DIM + 2 * SSD_HEADS

RET_HEADS = D_MODEL // 256
RET_QK_DIM = D_MODEL // RET_HEADS
RET_VALUE = 2 * D_MODEL
RET_V_DIM = RET_VALUE // RET_HEADS
RET_CHUNK = 128
RET_IN_DIM = 2 * D_MODEL + 2 * RET_VALUE
ROPE_BASE = 10000.0

MOE_GROUPS = 4
MOE_EXPERTS_PER_GROUP = 8
MOE_EXPERTS = MOE_GROUPS * MOE_EXPERTS_PER_GROUP
MOE_TOP_K = 2
MOE_HIDDEN = D_MODEL // 2
MOE_BLOCK = 128

kernel_name = 'hybrid_ssd_retention_hmoe_diffusion'


def swap01(t):
    return jnp.swapaxes(t, 0, 1)


def rev(t):
    return jnp.flip(t, axis=1)


def layer_norm(x, g, b):
    xf = x.astype(jnp.float32)
    mu = jnp.mean(xf, -1, keepdims=True)
    var = jnp.mean(jnp.square(xf - mu), -1, keepdims=True)
    return ((xf - mu) * lax.rsqrt(var + LN_EPS)).astype(x.dtype) * g + b


def modulate(x, shift, scale):
    return x * (1 + scale) + shift


def dwconv_centred(x, w, b):
    k = w.shape[0]
    y = lax.conv_general_dilated(x, w[:, None, :].astype(x.dtype), window_strides=(1,),
                                 padding=[(k // 2, k // 2)],
                                 dimension_numbers=('NWC', 'WIO', 'NWC'),
                                 feature_group_count=x.shape[-1])
    return y + b


def axial_rope_tables(n_rows, dtype):
    rows, cols = jnp.meshgrid(jnp.arange(n_rows), jnp.arange(GRID_W), indexing='ij')
    rows = rows.reshape(-1).astype(jnp.float32)
    cols = cols.reshape(-1).astype(jnp.float32)
    n_freq = RET_QK_DIM // 4
    inv_freq = ROPE_BASE ** (-jnp.arange(n_freq, dtype=jnp.float32) / n_freq)
    ang = jnp.concatenate([rows[:, None] * inv_freq, cols[:, None] * inv_freq], -1)
    return jnp.cos(ang).astype(dtype), jnp.sin(ang).astype(dtype)


def apply_rope(t, cos, sin):
    half = t.shape[-1] // 2
    t1, t2 = t[..., :half], t[..., half:]
    cs, sn = cos[None, :, None, :], sin[None, :, None, :]
    return jnp.concatenate([t1 * cs - t2 * sn, t1 * sn + t2 * cs], -1)


def ssd_chunk_scan(xdt, log_a, bm, cm, h0, want_y):
    bsz, seqlen = xdt.shape[:2]
    nc = seqlen // SSD_CHUNK
    dtype = xdt.dtype

    def chunk(t):
        return t.reshape((bsz, nc, SSD_CHUNK) + t.shape[2:])

    xc, bc, cc = chunk(xdt), chunk(bm), chunk(cm)
    acs = jnp.cumsum(chunk(log_a), axis=2)
    xe = xc * jnp.exp(acs[:, :, -1:] - acs).astype(dtype)[..., None]
    chunk_decay = jnp.exp(acs[:, :, -1]).astype(dtype)

    def state_step(h, b_c, xe_c, d_c):
        return h * d_c[..., None, None] + jnp.einsum('blgn,blghp->bghpn', b_c, xe_c)

    if not want_y:
        h_final, _ = lax.scan(lambda h, inp: (state_step(h, *inp), None), h0,
                              (swap01(bc), swap01(xe), swap01(chunk_decay)))
        return None, h_final

    def step(h, inp):
        b_c, xe_c, d_c, c_c, ea_c = inp
        y_off = jnp.einsum('blgn,bghpn->blghp', c_c, h) * ea_c[..., None]
        return state_step(h, b_c, xe_c, d_c), y_off

    ea = jnp.exp(acs).astype(dtype)
    h_final, y_off = lax.scan(step, h0, (swap01(bc), swap01(xe), swap01(chunk_decay),
                                         swap01(cc), swap01(ea)))
    pos = jnp.arange(SSD_CHUNK)
    lower = (pos[:, None] >= pos[None, :])[:, :, None, None]
    seg = acs[:, :, :, None] - acs[:, :, None, :]
    decay = jnp.exp(jnp.where(lower, seg, -jnp.inf)).astype(dtype)
    cb = jnp.einsum('bclgn,bcsgn->bclsg', cc, bc)
    y_diag = jnp.einsum('bclsgh,bcsghp->bclghp', cb[..., None] * decay, xc)
    return (y_diag + swap01(y_off)).reshape(xdt.shape), h_final


def ssd_branch(u, h0_f, h0_b, want_y, in_w, conv_w, conv_b, dt_bias, a_log, d_skip, norm_w):
    bsz, seqlen, _ = u.shape
    proj = u @ in_w
    xbc = jax.nn.silu(dwconv_centred(proj[..., SSD_D_INNER:SSD_D_INNER + SSD_CONV_DIM], conv_w, conv_b))
    dtype = xbc.dtype
    xs = xbc[..., :SSD_D_INNER].reshape(bsz, seqlen, SSD_GROUPS, SSD_HPG, SSD_HEADDIM)
    bm = xbc[..., SSD_D_INNER:SSD_D_INNER + SSD_BC_DIM].reshape(bsz, seqlen, SSD_GROUPS, SSD_STATE)
    cm = xbc[..., SSD_D_INNER + SSD_BC_DIM:].reshape(bsz, seqlen, SSD_GROUPS, SSD_STATE)
    dt_raw = proj[..., SSD_D_INNER + SSD_CONV_DIM:].astype(jnp.float32)
    dt_raw = dt_raw.reshape(bsz, seqlen, 2, SSD_GROUPS, SSD_HPG)
    dt = jax.nn.softplus(dt_raw + dt_bias.astype(jnp.float32).reshape(2, SSD_GROUPS, SSD_HPG))
    log_a = dt * -jnp.exp(a_log.astype(jnp.float32)).reshape(2, SSD_GROUPS, SSD_HPG)
    if h0_f is None:
        h0_f = h0_b = jnp.zeros((bsz, SSD_GROUPS, SSD_HPG, SSD_HEADDIM, SSD_STATE), dtype)
    y_f, h_f = ssd_chunk_scan(xs * dt[:, :, 0, ..., None].astype(dtype), log_a[:, :, 0],
                              bm, cm, h0_f.astype(dtype), want_y)
    y_b, h_b = ssd_chunk_scan(rev(xs * dt[:, :, 1, ..., None].astype(dtype)), rev(log_a[:, :, 1]),
                              rev(bm), rev(cm), h0_b.astype(dtype), want_y)
    if not want_y:
        return None, h_f, h_b
    y = y_f + rev(y_b) + xs * d_skip.reshape(SSD_GROUPS, SSD_HPG)[..., None]
    y = y.reshape(bsz, seqlen, SSD_D_INNER) * jax.nn.silu(proj[..., :SSD_D_INNER])
    yg = y.reshape(bsz, seqlen, SSD_GROUPS, SSD_D_INNER // SSD_GROUPS).astype(jnp.float32)
    yg = yg * lax.rsqrt(jnp.mean(jnp.square(yg), -1, keepdims=True) + LN_EPS)
    return yg.reshape(bsz, seqlen, SSD_D_INNER).astype(dtype) * norm_w, h_f, h_b


def retention_chunk_scan(q, k, v, log_gamma, s0, want_y):
    bsz, seqlen, nh, _ = q.shape
    nc = seqlen // RET_CHUNK
    dtype = q.dtype

    def chunk(t):
        return t.reshape((bsz, nc, RET_CHUNK) + t.shape[2:])

    qc, kc, vc = chunk(q), chunk(k), chunk(v)
    pos = jnp.arange(RET_CHUNK, dtype=jnp.float32)
    zeta = jnp.exp((RET_CHUNK - 1 - pos)[:, None] * log_gamma).astype(dtype)
    chunk_decay = jnp.exp(RET_CHUNK * log_gamma).astype(dtype)[:, None, None]
    kz = kc * zeta[:, :, None]

    def state_step(s, k_c, v_c):
        return s * chunk_decay + jnp.einsum('blhd,blhe->bhde', k_c, v_c)

    if not want_y:
        s_final, _ = lax.scan(lambda s, inp: (state_step(s, *inp), None), s0, (swap01(kz), swap01(vc)))
        return None, s_final

    xi = jnp.exp((pos + 1)[:, None] * log_gamma).astype(dtype)

    def step(s, inp):
        k_c, v_c, q_c = inp
        return state_step(s, k_c, v_c), jnp.einsum('blhd,bhde->blhe', q_c, s)

    s_final, cross = lax.scan(step, s0, (swap01(kz), swap01(vc), swap01(qc * xi[:, :, None])))
    rel = pos[:, None] - pos[None, :]
    dmat = jnp.exp(jnp.where((rel >= 0)[:, :, None], rel[:, :, None] * log_gamma, -jnp.inf)).astype(dtype)
    scores = jnp.einsum('bclhd,bcshd->bclsh', qc, kc) * dmat
    inner = jnp.einsum('bclsh,bcshe->bclhe', scores, vc)
    return (inner + swap01(cross)).reshape(bsz, seqlen, nh, v.shape[-1]), s_final


def retention_branch(u, rope, s0_f, s0_b, want_y, in_w, decay_logit, gn_w, gn_b):
    bsz, seqlen, _ = u.shape
    proj = u @ in_w
    dtype = proj.dtype
    qk_shape = (bsz, seqlen, RET_HEADS, RET_QK_DIM)
    q = proj[..., :D_MODEL].reshape(qk_shape)
    k = proj[..., D_MODEL:2 * D_MODEL].reshape(qk_shape) * (RET_QK_DIM ** -0.5)
    v = proj[..., 2 * D_MODEL:2 * D_MODEL + RET_VALUE].reshape(bsz, seqlen, RET_HEADS, RET_V_DIM)
    if rope is not None:
        q = apply_rope(q, *rope)
        k = apply_rope(k, *rope)
    log_gamma = jax.nn.log_sigmoid(decay_logit.astype(jnp.float32))
    if s0_f is None:
        s0_f = s0_b = jnp.zeros((bsz, RET_HEADS, RET_QK_DIM, RET_V_DIM), dtype)
    o_f, s_f = retention_chunk_scan(q, k, v, log_gamma[0], s0_f.astype(dtype), want_y)
    o_b, s_b = retention_chunk_scan(rev(q), rev(k), rev(v), log_gamma[1], s0_b.astype(dtype), want_y)
    if not want_y:
        return None, s_f, s_b
    o = (o_f + rev(o_b)).astype(jnp.float32)
    mu = jnp.mean(o, -1, keepdims=True)
    var = jnp.mean(jnp.square(o - mu), -1, keepdims=True)
    o = ((o - mu) * lax.rsqrt(var + LN_EPS)).reshape(bsz, seqlen, RET_VALUE).astype(dtype) * gn_w + gn_b
    return o * jax.nn.silu(proj[..., 2 * D_MODEL + RET_VALUE:]), s_f, s_b


def hier_moe(h, w_group, b_group, w_expert, b_expert, w_gate_up, w_down):
    n_tok, d = h.shape
    hf = h.astype(jnp.float32)
    g_logits = hf @ w_group.astype(jnp.float32) + b_group.astype(jnp.float32)
    g_sel = jnp.argmax(g_logits, -1)
    g_gate = jnp.take_along_axis(jax.nn.softmax(g_logits, -1), g_sel[:, None], 1)[:, 0]
    e_logits = (hf @ w_expert.astype(jnp.float32) + b_expert.astype(jnp.float32))
    e_logits = e_logits.reshape(n_tok, MOE_GROUPS, MOE_EXPERTS_PER_GROUP)
    e_logits = jnp.take_along_axis(e_logits, g_sel[:, None, None], 1)[:, 0]
    top_p, top_i = lax.top_k(jax.nn.softmax(e_logits, -1), MOE_TOP_K)
    top_w = top_p / jnp.sum(top_p, -1, keepdims=True) * g_gate[:, None]
    flat_e = (g_sel[:, None] * MOE_EXPERTS_PER_GROUP + top_i).reshape(-1)
    n_assign = n_tok * MOE_TOP_K
    order = jnp.argsort(flat_e)
    sorted_e = flat_e[order]
    counts = jnp.bincount(flat_e, length=MOE_EXPERTS)
    starts = jnp.cumsum(counts) - counts
    padded = (counts + MOE_BLOCK - 1) // MOE_BLOCK * MOE_BLOCK
    pends = jnp.cumsum(padded)
    dest_sorted = (pends - padded)[sorted_e] + jnp.arange(n_assign) - starts[sorted_e]
    dest = jnp.zeros((n_assign,), dest_sorted.dtype).at[order].set(dest_sorted)
    n_rows = -(-n_assign // MOE_BLOCK) * MOE_BLOCK + MOE_EXPERTS * MOE_BLOCK
    n_blocks = n_rows // MOE_BLOCK
    buf = jnp.zeros((n_rows, d), h.dtype).at[dest].set(jnp.repeat(h, MOE_TOP_K, axis=0))
    block_e = jnp.minimum(jnp.searchsorted(pends, jnp.arange(n_blocks) * MOE_BLOCK, side='right'),
                          MOE_EXPERTS - 1)

    def expert_block(args):
        xb, e = args
        gu = xb @ w_gate_up[e]
        return (jax.nn.silu(gu[:, :MOE_HIDDEN]) * gu[:, MOE_HIDDEN:]) @ w_down[e]

    out = lax.map(expert_block, (buf.reshape(n_blocks, MOE_BLOCK, d), block_e)).reshape(n_rows, -1)
    return jnp.sum(out[dest].reshape(n_tok, MOE_TOP_K, -1) * top_w[..., None].astype(out.dtype), axis=1)


def setup_inputs(seed: int = 0) -> dict:
    key = jax.random.key(seed)
    keys = iter(jax.random.split(key, 40))
    f32 = jnp.float32

    def normal(shape, scale):
        return jax.random.normal(next(keys), shape, f32) * scale

    def uniform(shape, lo, hi):
        return jax.random.uniform(next(keys), shape, f32, lo, hi)

    d = D_MODEL
    ns, nr = N_SSD_LAYERS, N_RET_LAYERS
    x = normal((BATCH, SEQ, d), 1.0)
    c = normal((BATCH, d), 1.0)
    ctx = normal((BATCH, CTX_LEN, d), 1.0)
    c_ctx = normal((d,), 1.0)
    mod_w = normal((DEPTH, d, 6 * d), 0.5 * d ** -0.5)
    mod_b = normal((DEPTH, 6 * d), 0.02)
    ssd_in_w = normal((ns, d, SSD_IN_DIM), d ** -0.5)
    ssd_conv_w = normal((ns, SSD_CONV_W, SSD_CONV_DIM), SSD_CONV_W ** -0.5)
    ssd_conv_b = normal((ns, SSD_CONV_DIM), 0.02)
    dt0 = jnp.exp(uniform((ns, 2, SSD_HEADS), math.log(1e-3), math.log(1e-1)))
    ssd_dt_bias = dt0 + jnp.log(-jnp.expm1(-dt0))
    ssd_a_log = jnp.log(uniform((ns, 2, SSD_HEADS), 1.0, 16.0))
    ssd_d_skip = 1.0 + normal((ns, SSD_HEADS), 0.02)
    ssd_norm_w = 1.0 + normal((ns, SSD_D_INNER), 0.02)
    ssd_out_w = normal((ns, SSD_D_INNER, d), SSD_D_INNER ** -0.5 * DEEPNORM_BETA)
    ret_in_w = normal((nr, d, RET_IN_DIM), d ** -0.5)
    m = 5.0 + jnp.arange(RET_HEADS, dtype=f32)
    ret_decay_logit = jnp.log(jnp.exp2(m) - 1.0) + normal((nr, 2, RET_HEADS), 0.1)
    ret_gn_w = 1.0 + normal((nr, RET_VALUE), 0.02)
    ret_gn_b = normal((nr, RET_VALUE), 0.02)
    ret_out_w = normal((nr, RET_VALUE, d), RET_VALUE ** -0.5 * DEEPNORM_BETA)
    ln_mix_g = 1.0 + normal((DEPTH, d), 0.02)
    ln_mix_b = normal((DEPTH, d), 0.02)
    ln_ffn_g = 1.0 + normal((DEPTH, d), 0.02)
    ln_ffn_b = normal((DEPTH, d), 0.02)
    moe_group_w = normal((DEPTH, d, MOE_GROUPS), d ** -0.5)
    moe_group_b = normal((DEPTH, MOE_GROUPS), 0.01)
    moe_expert_w = normal((DEPTH, d, MOE_EXPERTS), d ** -0.5)
    moe_expert_b = normal((DEPTH, MOE_EXPERTS), 0.01)
    moe_w_gate_up = normal((DEPTH, MOE_EXPERTS, d, 2 * MOE_HIDDEN), d ** -0.5)
    moe_w_down = normal((DEPTH, MOE_EXPERTS, MOE_HIDDEN, d), MOE_HIDDEN ** -0.5 * DEEPNORM_BETA)
    return {'x': x, 'c': c, 'ctx': ctx, 'c_ctx': c_ctx, 'mod_w': mod_w, 'mod_b': mod_b,
            'ssd_in_w': ssd_in_w, 'ssd_conv_w': ssd_conv_w, 'ssd_conv_b': ssd_conv_b,
            'ssd_dt_bias': ssd_dt_bias, 'ssd_a_log': ssd_a_log, 'ssd_d_skip': ssd_d_skip,
            'ssd_norm_w': ssd_norm_w, 'ssd_out_w': ssd_out_w,
            'ret_in_w': ret_in_w, 'ret_decay_logit': ret_decay_logit, 'ret_gn_w': ret_gn_w,
            'ret_gn_b': ret_gn_b, 'ret_out_w': ret_out_w,
            'ln_mix_g': ln_mix_g, 'ln_mix_b': ln_mix_b, 'ln_ffn_g': ln_ffn_g, 'ln_ffn_b': ln_ffn_b,
            'moe_group_w': moe_group_w, 'moe_group_b': moe_group_b, 'moe_expert_w': moe_expert_w,
            'moe_expert_b': moe_expert_b, 'moe_w_gate_up': moe_w_gate_up, 'moe_w_down': moe_w_down}


def reference(x, c, ctx, c_ctx, mod_w, mod_b, ssd_in_w, ssd_conv_w, ssd_conv_b, ssd_dt_bias,
              ssd_a_log, ssd_d_skip, ssd_norm_w, ssd_out_w, ret_in_w, ret_decay_logit, ret_gn_w,
              ret_gn_b, ret_out_w, ln_mix_g, ln_mix_b, ln_ffn_g, ln_ffn_b, moe_group_w, moe_group_b,
              moe_expert_w, moe_expert_b, moe_w_gate_up, moe_w_down):
    bsz, seqlen, d = x.shape
    n_lat = bsz * seqlen
    n_grid_rows = seqlen // GRID_W
    rope = axial_rope_tables(n_grid_rows, x.dtype)
    silu_c = jax.nn.silu(c)
    silu_cc = jax.nn.silu(c_ctx)
    for i in range(DEPTH):
        last = i == DEPTH - 1
        j = i // N_MIXERS
        sh1, sc1, g1, sh2, sc2, g2 = jnp.split((silu_c @ mod_w[i] + mod_b[i])[:, None, :], 6, axis=-1)
        csh1, csc1, cg1, csh2, csc2, cg2 = jnp.split(silu_cc @ mod_w[i] + mod_b[i], 6, axis=-1)
        u_ctx = modulate(ctx, csh1, csc1)
        u_lat = modulate(x, sh1, sc1)
        if i % N_MIXERS == 0:
            ssd_p = (ssd_in_w[j], ssd_conv_w[j], ssd_conv_b[j], ssd_dt_bias[j], ssd_a_log[j],
                     ssd_d_skip[j], ssd_norm_w[j])
            y_ctx, st_f, st_b = ssd_branch(u_ctx, None, None, not last, *ssd_p)
            y_lat, _, _ = ssd_branch(u_lat, st_f, st_b, True, *ssd_p)
            out_w = ssd_out_w[j]
        else:
            ret_p = (ret_in_w[j], ret_decay_logit[j], ret_gn_w[j], ret_gn_b[j])
            y_ctx, st_f, st_b = retention_branch(u_ctx, None, None, None, not last, *ret_p)
            y_lat, _, _ = retention_branch(u_lat, rope, st_f, st_b, True, *ret_p)
            out_w = ret_out_w[j]
        x = layer_norm(DEEPNORM_ALPHA * x + g1 * (y_lat @ out_w), ln_mix_g[i], ln_mix_b[i])
        tokens = modulate(x, sh2, sc2).reshape(n_lat, d)
        if not last:
            ctx = layer_norm(DEEPNORM_ALPHA * ctx + cg1 * (y_ctx @ out_w), ln_mix_g[i], ln_mix_b[i])
            tokens = jnp.concatenate([tokens, modulate(ctx, csh2, csc2).reshape(-1, d)], axis=0)
        ffn = hier_moe(tokens, moe_group_w[i], moe_group_b[i], moe_expert_w[i], moe_expert_b[i],
                       moe_w_gate_up[i], moe_w_down[i])
        x = layer_norm(DEEPNORM_ALPHA * x + g2 * ffn[:n_lat].reshape(x.shape), ln_ffn_g[i], ln_ffn_b[i])
        if not last:
            ctx = layer_norm(DEEPNORM_ALPHA * ctx + cg2 * ffn[n_lat:].reshape(ctx.shape),
                             ln_ffn_g[i], ln_ffn_b[i])
    return x
```

```python
import functools
import math

import jax
import jax.numpy as jnp
from jax import lax
from jax.experimental import pallas as pl
from jax.experimental.pallas import tpu as pltpu

F32 = jnp.float32
BF16 = jnp.bfloat16

D_MODEL = 1024
DEPTH = 4
GRID_W = 64
DEEPNORM_ALPHA = (2.0 * DEPTH) ** 0.25
LN_EPS = 1e-5

SSD_D_INNER = 2 * D_MODEL
SSD_HEADDIM = 64
SSD_HEADS = SSD_D_INNER // SSD_HEADDIM
SSD_GROUPS = 4
SSD_HPG = SSD_HEADS // SSD_GROUPS
SSD_STATE = 128
SSD_CONV_W = 5
SSD_BC_DIM = SSD_GROUPS * SSD_STATE
SSD_CONV_DIM = SSD_D_INNER + 2 * SSD_BC_DIM
SSD_GROUP_W = SSD_HPG * SSD_HEADDIM

RET_HEADS = D_MODEL // 256
RET_QK_DIM = D_MODEL // RET_HEADS
RET_VALUE = 2 * D_MODEL
RET_V_DIM = RET_VALUE // RET_HEADS
ROPE_BASE = 10000.0

MOE_GROUPS = 4
MOE_EXPERTS_PER_GROUP = 8
MOE_EXPERTS = MOE_GROUPS * MOE_EXPERTS_PER_GROUP
MOE_HIDDEN = D_MODEL // 2

CHUNK = 128
LANES = 128
ROW_BLOCK = 256
CONV_COLS = 256
VMEM_LIMIT = 48 * 1024 * 1024
NEG_BIG = -1e30


def _cparams(sem):
    return pltpu.CompilerParams(dimension_semantics=sem, vmem_limit_bytes=VMEM_LIMIT)


def _dot(a, b):
    return jnp.dot(a, b, preferred_element_type=F32)


def _split2(x):
    hi = x.astype(BF16)
    lo = (x - hi.astype(F32)).astype(BF16)
    return hi, lo


def _split3(x):
    hi = x.astype(BF16)
    r = x - hi.astype(F32)
    mid = r.astype(BF16)
    lo = (r - mid.astype(F32)).astype(BF16)
    return hi, mid, lo


def _dot_hi(a, b):
    ah, al = _split2(a)
    bh, bl = _split2(b)
    return _dot(ah, bh) + (_dot(ah, bl) + _dot(al, bh))


def _silu(x):
    return x * jax.nn.sigmoid(x)


def _softplus(x):
    return jnp.maximum(x, 0.0) + jnp.log1p(jnp.exp(-jnp.abs(x)))


def _mod_kernel(c_ref, w_ref, b_ref, o_ref):
    o_ref[...] = _dot_hi(_silu(c_ref[...]), w_ref[...]) + b_ref[...]


def _modulation(cc, mod_w, mod_b, layer):
    rows, d = cc.shape
    n = mod_w.shape[-1]
    tn = 1024
    out = pl.pallas_call(
        _mod_kernel,
        out_shape=jax.ShapeDtypeStruct((rows, n), F32),
        grid=(n // tn,),
        in_specs=[pl.BlockSpec((rows, d), lambda j: (0, 0)),
                  pl.BlockSpec((None, d, tn), lambda j: (layer, 0, j)),
                  pl.BlockSpec((None, 1, tn), lambda j: (layer, 0, j))],
        out_specs=pl.BlockSpec((rows, tn), lambda j: (0, j)),
        compiler_params=_cparams(("parallel",)),
        name="modulation",
    )(cc, mod_w, mod_b.reshape(mod_b.shape[0], 1, n))
    return out.reshape(rows, 6, d)


def _mod_spec(bsz):
    return pl.BlockSpec((1, 6, D_MODEL), lambda b, i: (jnp.where(i == 0, bsz, b), 0, 0))


def _ssd_proj_kernel(x_ref, mod_ref, w_ref, wdt_ref, dtb_ref, alog_ref, z_ref, xbc_ref, dtl_ref):
    m = mod_ref[0]
    u = (x_ref[0] * (1.0 + m[1:2, :]) + m[0:1, :]).astype(BF16)
    z_ref[0] = _dot(u, w_ref[:, 0:SSD_D_INNER])
    xbc_ref[0] = _dot(u, w_ref[:, SSD_D_INNER:SSD_D_INNER + SSD_CONV_DIM])
    dt = _softplus(_dot(u, wdt_ref[...]) + dtb_ref[...])
    lane = lax.broadcasted_iota(jnp.int32, dt.shape, 1)
    dtl_ref[0] = jnp.where(lane < 2 * SSD_HEADS, dt, dt * (-jnp.exp(alog_ref[...])))


def _ssd_proj(xc, mods, w_main, w_dt, dtb, alog, bsz):
    _, lt, d = xc.shape
    nblk = lt // ROW_BLOCK
    row = lambda w: pl.BlockSpec((1, ROW_BLOCK, w), lambda b, i: (b, i, 0))
    full = lambda a: pl.BlockSpec(a.shape, lambda b, i: (0,) * a.ndim)
    return pl.pallas_call(
        _ssd_proj_kernel,
        out_shape=(jax.ShapeDtypeStruct((bsz, lt, SSD_D_INNER), F32),
                   jax.ShapeDtypeStruct((bsz, lt, SSD_CONV_DIM), F32),
                   jax.ShapeDtypeStruct((bsz, lt, LANES), F32)),
        grid=(bsz, nblk),
        in_specs=[row(d), _mod_spec(bsz), full(w_main), full(w_dt), full(dtb), full(alog)],
        out_specs=(row(SSD_D_INNER), row(SSD_CONV_DIM), row(LANES)),
        compiler_params=_cparams(("parallel", "parallel")),
        name="ssd_in_proj",
    )(xc, mods, w_main, w_dt, dtb, alog)


def _conv_kernel(x_ref, w_ref, b_ref, o_ref, pad_ref, *, segments):
    halo = 8
    half = SSD_CONV_W // 2
    zeros = jnp.zeros((halo, x_ref.shape[-1]), F32)
    for s0, ln in segments:
        pad_ref[0:halo, :] = zeros
        pad_ref[halo:halo + ln, :] = x_ref[0, s0:s0 + ln, :]
        pad_ref[halo + ln:2 * halo + ln, :] = zeros
        for t in range(ln // ROW_BLOCK):
            r0 = t * ROW_BLOCK
            acc = b_ref[...] + w_ref[0:1, :] * pad_ref[r0 + halo - half:r0 + halo - half + ROW_BLOCK, :]
            for k in range(1, SSD_CONV_W):
                a = r0 + halo - half + k
                acc = acc + w_ref[k:k + 1, :] * pad_ref[a:a + ROW_BLOCK, :]
            o_ref[0, s0 + r0:s0 + r0 + ROW_BLOCK, :] = _silu(acc)


def _ssd_conv(xbc, conv_w, conv_b, ctx_len):
    bsz, lt, c = xbc.shape
    segments = ((0, ctx_len), (ctx_len, lt - ctx_len))
    blk = pl.BlockSpec((1, lt, CONV_COLS), lambda b, j: (b, 0, j))
    return pl.pallas_call(
        functools.partial(_conv_kernel, segments=segments),
        out_shape=jax.ShapeDtypeStruct((bsz, lt, c), F32),
        grid=(bsz, c // CONV_COLS),
        in_specs=[blk,
                  pl.BlockSpec((SSD_CONV_W, CONV_COLS), lambda b, j: (0, j)),
                  pl.BlockSpec((1, CONV_COLS), lambda b, j: (0, j))],
        out_specs=blk,
        scratch_shapes=[pltpu.VMEM((lt - ctx_len + 16, CONV_COLS), F32)],
        compiler_params=_cparams(("parallel", "parallel")),
        name="ssd_conv",
    )(xbc, conv_w, conv_b.reshape(1, c))


def _ssd_scan_kernel(*refs, direction, final):
    if final:
        xs_ref, b_ref, c_ref, dtl_ref, z_ref, yf_ref, dskip_ref, nw_ref, out_ref, h_ref, y_ref = refs
    else:
        xs_ref, b_ref, c_ref, dtl_ref, out_ref, h_ref = refs
        y_ref = out_ref.at[0]

    @pl.when(pl.program_id(1) == 0)
    def _():
        h_ref[...] = jnp.zeros(h_ref.shape, F32)

    row = lax.broadcasted_iota(jnp.int32, (CHUNK, CHUNK), 0)
    col = lax.broadcasted_iota(jnp.int32, (CHUNK, CHUNK), 1)
    if direction == 0:
        mask, end = row >= col, CHUNK - 1
    else:
        mask, end = col >= row, 0
    tri = jnp.where(mask, 1.0, 0.0).astype(BF16)
    tri_t = jnp.where(mask, 0.0, 1.0)
    tri_t = jnp.where(row == col, 1.0, tri_t).astype(BF16)
    lane_lo = col < SSD_HEADDIM

    dtl = dtl_ref[0]
    dtl_t = dtl.T
    p0, p1, p2 = _split3(dtl)
    cum = _dot(tri, p0) + _dot(tri, p1) + _dot(tri, p2)
    q0, q1, q2 = _split3(dtl_t)
    cum_t = _dot(q0, tri_t) + _dot(q1, tri_t) + _dot(q2, tri_t)
    o_dt = SSD_HEADS * direction
    o_la = 2 * SSD_HEADS + SSD_HEADS * direction
    dt_t = dtl_t[o_dt:o_dt + SSD_HEADS, :]
    a_t = cum_t[o_la:o_la + SSD_HEADS, :]
    w1_t = dt_t * jnp.exp(a_t[:, end:end + 1] - a_t)

    for g in range(SSD_GROUPS):
        gs = slice(g * SSD_STATE, (g + 1) * SSD_STATE)
        bg = b_ref[0, :, gs]
        cg = c_ref[0, :, gs].astype(BF16)
        cb = lax.dot_general(cg, bg.astype(BF16), (((1,), (1,)), ((), ())), preferred_element_type=F32)
        bg_t = bg.T
        h_in = h_ref[g]
        y_off = _dot(cg, h_in.astype(BF16))
        for jp in range(SSD_HPG // 2):
            ls = slice(g * SSD_GROUP_W + jp * LANES, g * SSD_GROUP_W + (jp + 1) * LANES)
            gl = slice(jp * LANES, (jp + 1) * LANES)
            xp = xs_ref[0, :, ls]
            top, bot, acols = [], [], []
            for e in range(2):
                h = g * SSD_HPG + 2 * jp + e
                a_col = jnp.broadcast_to(cum[:, o_la + h:o_la + h + 1], (CHUNK, CHUNK))
                seg = jnp.where(mask, a_col - a_t[h:h + 1, :], NEG_BIG)
                top.append(cb * jnp.exp(seg) * dt_t[h:h + 1, :])
                bot.append(bg_t * w1_t[h:h + 1, :])
                acols.append(a_col)
            lhs = jnp.concatenate([jnp.concatenate(top, axis=1), jnp.concatenate(bot, axis=1)], axis=0)
            rhs = jnp.concatenate([jnp.where(lane_lo, xp, 0.0), jnp.where(lane_lo, 0.0, xp)], axis=0)
            res = _dot(lhs.astype(BF16), rhs.astype(BF16))
            ea = jnp.exp(jnp.where(lane_lo, acols[0], acols[1]))
            y_ref[:, ls] = res[0:CHUNK] + y_off[:, gl] * ea
            h_ref[g, :, gl] = h_in[:, gl] * ea[end:end + 1, :] + res[CHUNK:2 * CHUNK]

    if final:
        y = y_ref[...] + yf_ref[0] + xs_ref[0] * dskip_ref[...]
        y = y * _silu(z_ref[0])
        gw = SSD_D_INNER // SSD_GROUPS
        for g in range(SSD_GROUPS):
            gs = slice(g * gw, (g + 1) * gw)
            yg = y[:, gs]
            ms = jnp.mean(yg * yg, axis=-1, keepdims=True)
            out_ref[0, :, gs] = (yg * lax.rsqrt(ms + LN_EPS) * nw_ref[:, gs]).astype(out_ref.dtype)


def _chunk_index(direction, ncc, nch):
    if direction == 0:
        return lambda j: j
    return lambda j: jnp.where(j < ncc, ncc - 1 - j, nch - 1 + ncc - j)


def _ssd_scan(xbc, dtl, direction, ctx_len, z=None, y_f=None, dskip=None, norm_w=None):
    bsz, lt, _ = xbc.shape
    nch, ncc = lt // CHUNK, ctx_len // CHUNK
    ci = _chunk_index(direction, ncc, nch)
    final = z is not None
    col = lambda w, cblk: pl.BlockSpec((1, CHUNK, w), lambda b, j: (b, ci(j), cblk))
    in_specs = [col(SSD_D_INNER, 0), col(SSD_BC_DIM, SSD_D_INNER // SSD_BC_DIM),
                col(SSD_BC_DIM, SSD_D_INNER // SSD_BC_DIM + 1), col(LANES, 0)]
    args = [xbc, xbc, xbc, dtl]
    scratch = [pltpu.VMEM((SSD_GROUPS, SSD_STATE, SSD_GROUP_W), F32)]
    if final:
        vec = pl.BlockSpec((1, SSD_D_INNER), lambda b, j: (0, 0))
        in_specs += [col(SSD_D_INNER, 0), col(SSD_D_INNER, 0), vec, vec]
        args += [z, y_f, dskip, norm_w]
        scratch.append(pltpu.VMEM((CHUNK, SSD_D_INNER), F32))
    return pl.pallas_call(
        functools.partial(_ssd_scan_kernel, direction=direction, final=final),
        out_shape=jax.ShapeDtypeStruct((bsz, lt, SSD_D_INNER), BF16 if final else F32),
        grid=(bsz, nch),
        in_specs=in_specs,
        out_specs=col(SSD_D_INNER, 0),
        scratch_shapes=scratch,
        compiler_params=_cparams(("parallel", "arbitrary")),
        name="ssd_scan_bwd" if final else "ssd_scan_fwd",
    )(*args)


def _ret_proj_kernel(x_ref, mod_ref, w_ref, cos_ref, sin_ref, q_ref, k_ref, v_ref, g_ref):
    m = mod_ref[0]
    u = (x_ref[0] * (1.0 + m[1:2, :]) + m[0:1, :]).astype(BF16)
    cs, sn = cos_ref[...], sin_ref[...]
    half = RET_QK_DIM // 2

    def rope(t, out_ref):
        for h in range(RET_HEADS):
            t1 = t[:, h * RET_QK_DIM:h * RET_QK_DIM + half]
            t2 = t[:, h * RET_QK_DIM + half:(h + 1) * RET_QK_DIM]
            out_ref[0, :, h * RET_QK_DIM:h * RET_QK_DIM + half] = t1 * cs - t2 * sn
            out_ref[0, :, h * RET_QK_DIM + half:(h + 1) * RET_QK_DIM] = t1 * sn + t2 * cs

    rope(_dot(u, w_ref[:, 0:D_MODEL]), q_ref)
    rope(_dot(u, w_ref[:, D_MODEL:2 * D_MODEL]) * (RET_QK_DIM ** -0.5), k_ref)
    v_ref[0] = _dot(u, w_ref[:, 2 * D_MODEL:2 * D_MODEL + RET_VALUE])
    g_ref[0] = _dot(u, w_ref[:, 2 * D_MODEL + RET_VALUE:2 * D_MODEL + 2 * RET_VALUE])


def _ret_proj(xc, mods, w, cos, sin, bsz):
    _, lt, d = xc.shape
    row = lambda wd: pl.BlockSpec((1, ROW_BLOCK, wd), lambda b, i: (b, i, 0))
    tab = pl.BlockSpec((ROW_BLOCK, RET_QK_DIM // 2), lambda b, i: (i, 0))
    return pl.pallas_call(
        _ret_proj_kernel,
        out_shape=(jax.ShapeDtypeStruct((bsz, lt, D_MODEL), F32),
                   jax.ShapeDtypeStruct((bsz, lt, D_MODEL), F32),
                   jax.ShapeDtypeStruct((bsz, lt, RET_VALUE), F32),
                   jax.ShapeDtypeStruct((bsz, lt, RET_VALUE), F32)),
        grid=(bsz, lt // ROW_BLOCK),
        in_specs=[row(d), _mod_spec(bsz), pl.BlockSpec(w.shape, lambda b, i: (0, 0)), tab, tab],
        out_specs=(row(D_MODEL), row(D_MODEL), row(RET_VALUE), row(RET_VALUE)),
        compiler_params=_cparams(("parallel", "parallel")),
        name="ret_in_proj",
    )(xc, mods, w, cos, sin)


def _ret_scan_kernel(*refs, final):
    if final:
        (q_ref, k_ref, v_ref, xi_ref, zeta_ref, dmat_ref, cdec_ref,
         g_ref, of_ref, gnw_ref, gnb_ref, out_ref, s_ref) = refs
    else:
        q_ref, k_ref, v_ref, xi_ref, zeta_ref, dmat_ref, cdec_ref, out_ref, s_ref = refs

    @pl.when(pl.program_id(1) == 0)
    def _():
        s_ref[...] = jnp.zeros(s_ref.shape, F32)

    for h in range(RET_HEADS):
        qs = slice(h * RET_QK_DIM, (h + 1) * RET_QK_DIM)
        vs = slice(h * RET_V_DIM, (h + 1) * RET_V_DIM)
        qh, kh = q_ref[0, :, qs], k_ref[0, :, qs]
        vh = v_ref[0, :, vs].astype(BF16)
        s_in = s_ref[h]
        sc = lax.dot_general(qh.astype(BF16), kh.astype(BF16), (((1,), (1,)), ((), ())),
                             preferred_element_type=F32) * dmat_ref[h]
        o = _dot(sc.astype(BF16), vh) + _dot((qh * xi_ref[:, qs]).astype(BF16), s_in.astype(BF16))
        kz_t = (kh * zeta_ref[:, qs]).T
        s_ref[h] = s_in * cdec_ref[h] + _dot(kz_t.astype(BF16), vh)
        if final:
            o = o + of_ref[0, :, vs]
            mu = jnp.mean(o, axis=-1, keepdims=True)
            var = jnp.mean(jnp.square(o - mu), axis=-1, keepdims=True)
            o = (o - mu) * lax.rsqrt(var + LN_EPS) * gnw_ref[:, vs] + gnb_ref[:, vs]
            out_ref[0, :, vs] = (o * _silu(g_ref[0, :, vs])).astype(out_ref.dtype)
        else:
            out_ref[0, :, vs] = o


def _ret_scan(q, k, v, tabs, direction, ctx_len, g=None, o_f=None, gn_w=None, gn_b=None):
    bsz, lt, _ = q.shape
    nch, ncc = lt // CHUNK, ctx_len // CHUNK
    ci = _chunk_index(direction, ncc, nch)
    final = g is not None
    xi, zeta, dmat, cdec = tabs
    col = lambda w: pl.BlockSpec((1, CHUNK, w), lambda b, j: (b, ci(j), 0))
    full = lambda a: pl.BlockSpec(a.shape, lambda b, j: (0,) * a.ndim)
    in_specs = [col(D_MODEL), col(D_MODEL), col(RET_VALUE), full(xi), full(zeta), full(dmat), full(cdec)]
    args = [q, k, v, xi, zeta, dmat, cdec]
    if final:
        vec = pl.BlockSpec((1, RET_VALUE), lambda b, j: (0, 0))
        in_specs += [col(RET_VALUE), col(RET_VALUE), vec, vec]
        args += [g, o_f, gn_w, gn_b]
    return pl.pallas_call(
        functools.partial(_ret_scan_kernel, final=final),
        out_shape=jax.ShapeDtypeStruct((bsz, lt, RET_VALUE), BF16 if final else F32),
        grid=(bsz, nch),
        in_specs=in_specs,
        out_specs=col(RET_VALUE),
        scratch_shapes=[pltpu.VMEM((RET_HEADS, RET_QK_DIM, RET_V_DIM), F32)],
        compiler_params=_cparams(("parallel", "arbitrary")),
        name="ret_scan_bwd" if final else "ret_scan_fwd",
    )(*args)


def _ret_tables(decay_logit):
    lg = jax.nn.log_sigmoid(decay_logit.astype(F32))
    pos = jnp.arange(CHUNK, dtype=F32)
    rel = pos[:, None] - pos[None, :]
    out = []
    for d in range(2):
        l = lg[d]
        if d == 0:
            xi_e, zeta_e, r = pos + 1.0, CHUNK - 1.0 - pos, rel
        else:
            xi_e, zeta_e, r = CHUNK - pos, pos, -rel
        xi = jnp.repeat(jnp.exp(xi_e[:, None] * l), RET_QK_DIM, axis=1)
        zeta = jnp.repeat(jnp.exp(zeta_e[:, None] * l), RET_QK_DIM, axis=1)
        dmat = jnp.exp(jnp.where((r >= 0)[None], r[None] * l[:, None, None], -jnp.inf))
        cdec = jnp.broadcast_to(jnp.exp(CHUNK * l)[:, None, None], (RET_HEADS, 1, RET_V_DIM))
        out.append((xi, zeta, dmat, cdec))
    return out


def _rope_tables(n_rows, ctx_len):
    rows, cols = jnp.meshgrid(jnp.arange(n_rows), jnp.arange(GRID_W), indexing='ij')
    rows = rows.reshape(-1).astype(F32)
    cols = cols.reshape(-1).astype(F32)
    n_freq = RET_QK_DIM // 4
    inv_freq = ROPE_BASE ** (-jnp.arange(n_freq, dtype=F32) / n_freq)
    ang = jnp.concatenate([rows[:, None] * inv_freq, cols[:, None] * inv_freq], -1)
    cos = jnp.concatenate([jnp.ones((ctx_len, ang.shape[1]), F32), jnp.cos(ang)], 0)
    sin = jnp.concatenate([jnp.zeros((ctx_len, ang.shape[1]), F32), jnp.sin(ang)], 0)
    return cos, sin


def _layer_norm(t, g, b):
    mu = jnp.mean(t, axis=-1, keepdims=True)
    var = jnp.mean(jnp.square(t - mu), axis=-1, keepdims=True)
    return (t - mu) * lax.rsqrt(var + LN_EPS) * g + b


def _out_kernel(y_ref, x_ref, mod_ref, w_ref, lng_ref, lnb_ref, xn_ref, tok_ref):
    m = mod_ref[0]
    o = _dot(y_ref[0], w_ref[...])
    xn = _layer_norm(DEEPNORM_ALPHA * x_ref[0] + m[2:3, :] * o, lng_ref[...], lnb_ref[...])
    xn_ref[0] = xn
    tok_ref[0] = xn * (1.0 + m[4:5, :]) + m[3:4, :]


def _out_proj(y, xc, mods, w, ln_g, ln_b, bsz):
    _, lt, d = xc.shape
    row = lambda wd: pl.BlockSpec((1, ROW_BLOCK, wd), lambda b, i: (b, i, 0))
    vec = pl.BlockSpec((1, d), lambda b, i: (0, 0))
    return pl.pallas_call(
        _out_kernel,
        out_shape=(jax.ShapeDtypeStruct((bsz, lt, d), F32), jax.ShapeDtypeStruct((bsz, lt, d), F32)),
        grid=(bsz, lt // ROW_BLOCK),
        in_specs=[row(y.shape[-1]), row(d), _mod_spec(bsz), pl.BlockSpec(w.shape, lambda b, i: (0, 0)),
                  vec, vec],
        out_specs=(row(d), row(d)),
        compiler_params=_cparams(("parallel", "parallel")),
        name="out_proj_norm",
    )(y, xc, mods, w, ln_g, ln_b)


_META_E, _META_RANK, _META_W = 0, 2, 4


def _router_kernel(tok_ref, w_ref, b_ref, meta_ref, cnt_ref, carry_ref):
    @pl.when(pl.program_id(0) == 0)
    def _():
        carry_ref[...] = jnp.zeros(carry_ref.shape, F32)

    logits = _dot_hi(tok_ref[...], w_ref[...]) + b_ref[...]
    shape = logits.shape
    lane = lax.broadcasted_iota(jnp.int32, shape, 1).astype(F32)
    big = float(LANES)

    def first_max(vals):
        m = jnp.max(vals, axis=-1, keepdims=True)
        return m, jnp.min(jnp.where(vals == m, lane, big), axis=-1, keepdims=True)

    gl = jnp.where(lane < MOE_GROUPS, logits, -jnp.inf)
    gmax, gsel = first_max(gl)
    g_gate = 1.0 / jnp.sum(jnp.exp(gl - gmax), axis=-1, keepdims=True)
    lo = MOE_GROUPS + gsel * MOE_EXPERTS_PER_GROUP
    el = jnp.where((lane >= lo) & (lane < lo + MOE_EXPERTS_PER_GROUP), logits, -jnp.inf)
    m1, i1 = first_max(el)
    m2, i2 = first_max(jnp.where(lane == i1, -jnp.inf, el))
    e2 = jnp.exp(m2 - m1)
    w1 = g_gate / (1.0 + e2)
    w2 = g_gate * e2 / (1.0 + e2)
    e1, e2id = i1 - MOE_GROUPS, i2 - MOE_GROUPS

    oh1 = jnp.where(lane == e1, 1.0, 0.0)
    oh2 = jnp.where(lane == e2id, 1.0, 0.0)
    ohs = oh1 + oh2
    n = shape[0]
    r = lax.broadcasted_iota(jnp.int32, (n, n), 0)
    c = lax.broadcasted_iota(jnp.int32, (n, n), 1)
    before = _dot(jnp.where(c < r, 1.0, 0.0).astype(BF16), ohs.astype(BF16)) + carry_ref[...]
    rank1 = jnp.sum(oh1 * before, axis=-1, keepdims=True)
    rank2 = jnp.sum(oh2 * before, axis=-1, keepdims=True)
    carry_ref[...] = carry_ref[...] + jnp.sum(ohs, axis=0, keepdims=True)
    cnt_ref[...] = carry_ref[...]

    rec = jnp.zeros(shape, F32)
    for k, val in enumerate((e1, e2id, rank1, rank2, w1, w2)):
        rec = jnp.where(lane == float(k), val, rec)
    meta_ref[...] = rec


def _router(tokens, w_r, b_r):
    t, d = tokens.shape
    return pl.pallas_call(
        _router_kernel,
        out_shape=(jax.ShapeDtypeStruct((t, LANES), F32), jax.ShapeDtypeStruct((1, LANES), F32)),
        grid=(t // ROW_BLOCK,),
        in_specs=[pl.BlockSpec((ROW_BLOCK, d), lambda i: (i, 0)),
                  pl.BlockSpec((d, LANES), lambda i: (0, 0)),
                  pl.BlockSpec((1, LANES), lambda i: (0, 0))],
        out_specs=(pl.BlockSpec((ROW_BLOCK, LANES), lambda i: (i, 0)),
                   pl.BlockSpec((1, LANES), lambda i: (0, 0))),
        scratch_shapes=[pltpu.VMEM((1, LANES), F32)],
        compiler_params=_cparams(("arbitrary",)),
        name="moe_router",
    )(tokens, w_r, b_r)


def _row_copy(src_ref, src_row, dst_ref, dst_row, sem):
    return pltpu.make_async_copy(src_ref.at[pl.ds(src_row, 1), :], dst_ref.at[pl.ds(dst_row, 1), :], sem)


def _dispatch_kernel(dest_ref, tok_ref, buf_in_ref, buf_ref, sem):
    del buf_in_ref
    base = pl.program_id(0) * ROW_BLOCK

    def issue(r, carry):
        for k in range(2):
            _row_copy(tok_ref, r, buf_ref, dest_ref[(base + r) * 2 + k], sem).start()
        return carry

    lax.fori_loop(0, ROW_BLOCK, issue, 0)

    def drain(r, carry):
        for k in range(2):
            _row_copy(tok_ref, r, buf_ref, dest_ref[(base + r) * 2 + k], sem).wait()
        return carry

    lax.fori_loop(0, ROW_BLOCK, drain, 0)


def _dispatch(dest_flat, tokens, n_rows):
    t, d = tokens.shape
    buf0 = jnp.zeros((n_rows, d), F32)
    return pl.pallas_call(
        _dispatch_kernel,
        out_shape=jax.ShapeDtypeStruct((n_rows, d), F32),
        grid_spec=pltpu.PrefetchScalarGridSpec(
            num_scalar_prefetch=1,
            grid=(t // ROW_BLOCK,),
            in_specs=[pl.BlockSpec((ROW_BLOCK, d), lambda i, dest: (i, 0)),
                      pl.BlockSpec(memory_space=pl.ANY)],
            out_specs=pl.BlockSpec(memory_space=pl.ANY),
            scratch_shapes=[pltpu.SemaphoreType.DMA],
        ),
        input_output_aliases={2: 0},
        compiler_params=_cparams(("arbitrary",)),
        name="moe_dispatch",
    )(dest_flat, tokens, buf0)


def _expert_kernel(be_ref, nused_ref, x_ref, wgu_ref, wd_ref, o_ref):
    del be_ref
    i = pl.program_id(0)

    @pl.when(i < nused_ref[0])
    def _():
        gu = _dot(x_ref[...].astype(BF16), wgu_ref[...])
        act = _silu(gu[:, :MOE_HIDDEN]) * gu[:, MOE_HIDDEN:]
        o_ref[...] = _dot(act.astype(BF16), wd_ref[...])

    @pl.when(i >= nused_ref[0])
    def _():
        o_ref[...] = jnp.zeros(o_ref.shape, F32)


def _experts(block_e, n_used, buf, w_gu, w_d, layer):
    n_rows, d = buf.shape
    return pl.pallas_call(
        _expert_kernel,
        out_shape=jax.ShapeDtypeStruct((n_rows, d), F32),
        grid_spec=pltpu.PrefetchScalarGridSpec(
            num_scalar_prefetch=2,
            grid=(n_rows // ROW_BLOCK,),
            in_specs=[pl.BlockSpec((ROW_BLOCK, d), lambda i, be, nu: (i, 0)),
                      pl.BlockSpec((None, None, d, 2 * MOE_HIDDEN), lambda i, be, nu: (layer, be[i], 0, 0)),
                      pl.BlockSpec((None, None, MOE_HIDDEN, d), lambda i, be, nu: (layer, be[i], 0, 0))],
            out_specs=pl.BlockSpec((ROW_BLOCK, d), lambda i, be, nu: (i, 0)),
        ),
        compiler_params=_cparams(("arbitrary",)),
        name="moe_experts",
    )(block_e, n_used, buf, w_gu, w_d)


def _combine_kernel(dest_ref, eo_ref, meta_ref, x_ref, mod_ref, lng_ref, lnb_ref, out_ref, g_ref, sem):
    nblk = pl.num_programs(1)
    base = (pl.program_id(0) * nblk + pl.program_id(1)) * ROW_BLOCK

    def issue(r, carry):
        for k in range(2):
            _row_copy(eo_ref, dest_ref[(base + r) * 2 + k], g_ref.at[k], r, sem).start()
        return carry

    lax.fori_loop(0, ROW_BLOCK, issue, 0)

    def drain(r, carry):
        for k in range(2):
            _row_copy(eo_ref, dest_ref[(base + r) * 2 + k], g_ref.at[k], r, sem).wait()
        return carry

    lax.fori_loop(0, ROW_BLOCK, drain, 0)

    meta = meta_ref[...]
    ffn = g_ref[0] * meta[:, _META_W:_META_W + 1] + g_ref[1] * meta[:, _META_W + 1:_META_W + 2]
    m = mod_ref[0]
    out_ref[0] = _layer_norm(DEEPNORM_ALPHA * x_ref[0] + m[5:6, :] * ffn, lng_ref[...], lnb_ref[...])


def _combine(dest_flat, eo, meta, xn, mods, ln_g, ln_b, bsz):
    _, lt, d = xn.shape
    nblk = lt // ROW_BLOCK
    return pl.pallas_call(
        _combine_kernel,
        out_shape=jax.ShapeDtypeStruct((bsz, lt, d), F32),
        grid_spec=pltpu.PrefetchScalarGridSpec(
            num_scalar_prefetch=1,
            grid=(bsz, nblk),
            in_specs=[pl.BlockSpec(memory_space=pl.ANY),
                      pl.BlockSpec((ROW_BLOCK, LANES), lambda b, i, dest: (b * nblk + i, 0)),
                      pl.BlockSpec((1, ROW_BLOCK, d), lambda b, i, dest: (b, i, 0)),
                      pl.BlockSpec((1, 6, d), lambda b, i, dest: (jnp.where(i == 0, bsz, b), 0, 0)),
                      pl.BlockSpec((1, d), lambda b, i, dest: (0, 0)),
                      pl.BlockSpec((1, d), lambda b, i, dest: (0, 0))],
            out_specs=pl.BlockSpec((1, ROW_BLOCK, d), lambda b, i, dest: (b, i, 0)),
            scratch_shapes=[pltpu.VMEM((2, ROW_BLOCK, d), F32), pltpu.SemaphoreType.DMA],
        ),
        compiler_params=_cparams(("arbitrary", "arbitrary")),
        name="moe_combine",
    )(dest_flat, eo, meta, xn, mods, ln_g, ln_b)


def _moe(tokens3, xn, mods, w_r, b_r, w_gu, w_d, layer, ln_g, ln_b, bsz):
    _, lt, d = tokens3.shape
    tokens = tokens3.reshape(bsz * lt, d)
    t = tokens.shape[0]
    meta, cnt = _router(tokens, w_r, b_r)
    eid = meta[:, _META_E:_META_E + 2].astype(jnp.int32)
    rank = meta[:, _META_RANK:_META_RANK + 2].astype(jnp.int32)
    counts = cnt[0, :MOE_EXPERTS].astype(jnp.int32)
    padded = (counts + ROW_BLOCK - 1) // ROW_BLOCK * ROW_BLOCK
    pends = jnp.cumsum(padded)
    dest_flat = (jnp.take(pends - padded, eid) + rank).reshape(-1)
    n_rows = (2 * t + MOE_EXPERTS * (ROW_BLOCK - 1)) // ROW_BLOCK * ROW_BLOCK
    n_blocks = n_rows // ROW_BLOCK
    block_e = jnp.minimum(jnp.searchsorted(pends, jnp.arange(n_blocks, dtype=jnp.int32) * ROW_BLOCK,
                                           side='right'), MOE_EXPERTS - 1).astype(jnp.int32)
    n_used = (pends[-1:] // ROW_BLOCK).astype(jnp.int32)
    buf = _dispatch(dest_flat, tokens, n_rows)
    eo = _experts(block_e, n_used, buf, w_gu, w_d, layer)
    return _combine(dest_flat, eo, meta, xn, mods, ln_g, ln_b, bsz)


def kernel(x, c, ctx, c_ctx, mod_w, mod_b, ssd_in_w, ssd_conv_w, ssd_conv_b, ssd_dt_bias, ssd_a_log, ssd_d_skip, ssd_norm_w, ssd_out_w, ret_in_w, ret_decay_logit, ret_gn_w, ret_gn_b, ret_out_w, ln_mix_g, ln_mix_b, ln_ffn_g, ln_ffn_b, moe_group_w, moe_group_b, moe_expert_w, moe_expert_b, moe_w_gate_up, moe_w_down):
    bsz, seqlen, d = x.shape
    ctx_len = ctx.shape[1]
    assert d == D_MODEL and ctx_len == ROW_BLOCK and seqlen % ROW_BLOCK == 0 and seqlen % GRID_W == 0
    assert bsz + 1 <= 16
    depth = mod_w.shape[0]

    xc = jnp.concatenate([ctx, x], axis=1)
    cc = jnp.zeros((16, d), F32).at[:bsz].set(c).at[bsz].set(c_ctx)
    cos, sin = _rope_tables(seqlen // GRID_W, ctx_len)
    w_gu = moe_w_gate_up.astype(BF16)
    w_d = moe_w_down.astype(BF16)

    for i in range(depth):
        j = i // 2
        mods = _modulation(cc, mod_w, mod_b, i)
        if i % 2 == 0:
            w = ssd_in_w[j]
            n_main = SSD_D_INNER + SSD_CONV_DIM
            w_main = w[:, :n_main].astype(BF16)
            w_dt = jnp.concatenate([w[:, n_main:], w[:, n_main:]], axis=1).astype(BF16)
            dtb = jnp.tile(ssd_dt_bias[j].reshape(1, -1), (1, 2))
            alog = jnp.tile(ssd_a_log[j].reshape(1, -1), (1, 2))
            z, xbc, dtl = _ssd_proj(xc, mods, w_main, w_dt, dtb, alog, bsz)
            xbc = _ssd_conv(xbc, ssd_conv_w[j], ssd_conv_b[j], ctx_len)
            y_f = _ssd_scan(xbc, dtl, 0, ctx_len)
            dskip = jnp.repeat(ssd_d_skip[j], SSD_HEADDIM).reshape(1, -1)
            y = _ssd_scan(xbc, dtl, 1, ctx_len, z=z, y_f=y_f, dskip=dskip,
                          norm_w=ssd_norm_w[j].reshape(1, -1))
            out_w = ssd_out_w[j]
        else:
            q, k, v, g = _ret_proj(xc, mods, ret_in_w[j].astype(BF16), cos, sin, bsz)
            tabs = _ret_tables(ret_decay_logit[j])
            o_f = _ret_scan(q, k, v, tabs[0], 0, ctx_len)
            y = _ret_scan(q, k, v, tabs[1], 1, ctx_len, g=g, o_f=o_f,
                          gn_w=ret_gn_w[j].reshape(1, -1), gn_b=ret_gn_b[j].reshape(1, -1))
            out_w = ret_out_w[j]
        xn, tokens = _out_proj(y, xc, mods, out_w.astype(BF16), ln_mix_g[i].reshape(1, -1),
                               ln_mix_b[i].reshape(1, -1), bsz)
        w_r = jnp.zeros((d, LANES), F32).at[:, :MOE_GROUPS].set(moe_group_w[i])
        w_r = w_r.at[:, MOE_GROUPS:MOE_GROUPS + MOE_EXPERTS].set(moe_expert_w[i])
        b_r = jnp.zeros((1, LANES), F32).at[0, :MOE_GROUPS].set(moe_group_b[i])
        b_r = b_r.at[0, MOE_GROUPS:MOE_GROUPS + MOE_EXPERTS].set(moe_expert_b[i])
        xc = _moe(tokens, xn, mods, w_r, b_r, w_gu, w_d, i, ln_ffn_g[i].reshape(1, -1),
                  ln_ffn_b[i].reshape(1, -1), bsz)
    return xc[:, ctx_len:, :]
```

```python
import functools

import jax
import jax.numpy as jnp
from jax import lax
from jax.experimental import pallas as pl
from jax.experimental.pallas import tpu as pltpu

F32 = jnp.float32
BF16 = jnp.bfloat16

D_MODEL = 1024
DEPTH = 4
GRID_W = 64
DEEPNORM_ALPHA = (2.0 * DEPTH) ** 0.25
LN_EPS = 1e-5

SSD_D_INNER = 2 * D_MODEL
SSD_HEADDIM = 64
SSD_HEADS = SSD_D_INNER // SSD_HEADDIM
SSD_GROUPS = 4
SSD_HPG = SSD_HEADS // SSD_GROUPS
SSD_STATE = 128
SSD_CONV_W = 5
SSD_BC_DIM = SSD_GROUPS * SSD_STATE
SSD_CONV_DIM = SSD_D_INNER + 2 * SSD_BC_DIM
SSD_GROUP_W = SSD_HPG * SSD_HEADDIM

RET_HEADS = D_MODEL // 256
RET_QK_DIM = D_MODEL // RET_HEADS
RET_VALUE = 2 * D_MODEL
RET_V_DIM = RET_VALUE // RET_HEADS
ROPE_BASE = 10000.0

MOE_GROUPS = 4
MOE_EXPERTS_PER_GROUP = 8
MOE_EXPERTS = MOE_GROUPS * MOE_EXPERTS_PER_GROUP
MOE_HIDDEN = D_MODEL // 2
MOE_TOP_K = 2

CHUNK = 128
LANES = 128
ROW_BLOCK = 256
SCAN_SUB = ROW_BLOCK // CHUNK
CONV_HALO = 8
CONV_COLS = 512
ROW_DMA_UNROLL = 8
VMEM_LIMIT = 48 * 1024 * 1024
NEG_BIG = -1e30


def _cparams(sem):
    return pltpu.CompilerParams(dimension_semantics=sem, vmem_limit_bytes=VMEM_LIMIT)


def _dot(a, b):
    return jnp.dot(a, b, preferred_element_type=F32)


def _split2(x):
    hi = x.astype(BF16)
    lo = (x - hi.astype(F32)).astype(BF16)
    return hi, lo


def _split3(x):
    hi = x.astype(BF16)
    r = x - hi.astype(F32)
    mid = r.astype(BF16)
    lo = (r - mid.astype(F32)).astype(BF16)
    return hi, mid, lo


def _dot_hi(a, b):
    ah, al = _split2(a)
    bh, bl = _split2(b)
    return _dot(ah, bh) + (_dot(ah, bl) + _dot(al, bh))


def _silu(x):
    return x * jax.nn.sigmoid(x)


def _softplus(x):
    return jnp.maximum(x, 0.0) + jnp.log1p(jnp.exp(-jnp.abs(x)))


def _mod_kernel(c_ref, w_ref, b_ref, o_ref):
    o_ref[...] = _dot_hi(_silu(c_ref[...]), w_ref[...]) + b_ref[...]


def _modulation(cc, mod_w, mod_b, layer):
    rows, d = cc.shape
    n = mod_w.shape[-1]
    tn = 1024
    out = pl.pallas_call(
        _mod_kernel,
        out_shape=jax.ShapeDtypeStruct((rows, n), F32),
        grid=(n // tn,),
        in_specs=[pl.BlockSpec((rows, d), lambda j: (0, 0)),
                  pl.BlockSpec((None, d, tn), lambda j: (layer, 0, j)),
                  pl.BlockSpec((None, 1, tn), lambda j: (layer, 0, j))],
        out_specs=pl.BlockSpec((rows, tn), lambda j: (0, j)),
        compiler_params=_cparams(("parallel",)),
        name="modulation",
    )(cc, mod_w, mod_b.reshape(mod_b.shape[0], 1, n))
    return out.reshape(rows, 6, d)


def _mod_spec(bsz):
    return pl.BlockSpec((1, 6, D_MODEL), lambda b, i: (jnp.where(i == 0, bsz, b), 0, 0))


def _ssd_proj_kernel(x_ref, xlo_ref, xhi_ref, mod_ref, w_ref, wdt_ref, dtb_ref, alog_ref, cw_ref, cb_ref,
                     z_ref, xbc_ref, dtl_ref, pad_ref):
    i = pl.program_id(1)
    m = mod_ref[0]
    main = slice(CONV_HALO, CONV_HALO + ROW_BLOCK)
    n_all = ROW_BLOCK + 2 * CONV_HALO
    x_all = jnp.concatenate([xlo_ref[0], x_ref[0], xhi_ref[0]], axis=0)
    u = (x_all * (1.0 + m[1:2, :]) + m[0:1, :]).astype(BF16)
    z_ref[0] = _dot(u, w_ref[:, 0:SSD_D_INNER])[main]
    r = lax.broadcasted_iota(jnp.int32, (n_all, 1), 0)
    lo_ok = jnp.where(i >= 2, 1.0, 0.0)
    hi_ok = jnp.where((i >= 1) & (i < pl.num_programs(1) - 1), 1.0, 0.0)
    keep = jnp.where(r < CONV_HALO, lo_ok, jnp.where(r >= CONV_HALO + ROW_BLOCK, hi_ok, 1.0))
    pad_ref[...] = _dot(u, w_ref[:, SSD_D_INNER:SSD_D_INNER + SSD_CONV_DIM]) * keep
    half = SSD_CONV_W // 2
    for c0 in range(0, SSD_CONV_DIM, CONV_COLS):
        cs = slice(c0, c0 + CONV_COLS)
        v = pad_ref[:, cs]
        acc = cb_ref[:, cs] + cw_ref[half:half + 1, cs] * v[main]
        for k in range(SSD_CONV_W):
            if k != half:
                acc = acc + cw_ref[k:k + 1, cs] * pltpu.roll(v, (half - k) % n_all, 0)[main]
        xbc_ref[0, :, cs] = _silu(acc)
    dt = _softplus(_dot(u, wdt_ref[...])[main] + dtb_ref[...])
    lane = lax.broadcasted_iota(jnp.int32, dt.shape, 1)
    dtl_ref[0] = jnp.where(lane < 2 * SSD_HEADS, dt, dt * (-jnp.exp(alog_ref[...])))


def _ssd_proj(xc, mods, w_main, w_dt, dtb, alog, conv_w, conv_b, bsz):
    _, lt, d = xc.shape
    nblk = lt // ROW_BLOCK
    per = ROW_BLOCK // CONV_HALO
    row = lambda w: pl.BlockSpec((1, ROW_BLOCK, w), lambda b, i: (b, i, 0))
    full = lambda a: pl.BlockSpec(a.shape, lambda b, i: (0,) * a.ndim)
    lo = pl.BlockSpec((1, CONV_HALO, d), lambda b, i: (b, jnp.maximum(i * per - 1, 0), 0))
    hi = pl.BlockSpec((1, CONV_HALO, d), lambda b, i: (b, jnp.minimum((i + 1) * per, nblk * per - 1), 0))
    return pl.pallas_call(
        _ssd_proj_kernel,
        out_shape=(jax.ShapeDtypeStruct((bsz, lt, SSD_D_INNER), F32),
                   jax.ShapeDtypeStruct((bsz, lt, SSD_CONV_DIM), F32),
                   jax.ShapeDtypeStruct((bsz, lt, LANES), F32)),
        grid=(bsz, nblk),
        in_specs=[row(d), lo, hi, _mod_spec(bsz), full(w_main), full(w_dt), full(dtb), full(alog),
                  full(conv_w), full(conv_b)],
        out_specs=(row(SSD_D_INNER), row(SSD_CONV_DIM), row(LANES)),
        scratch_shapes=[pltpu.VMEM((ROW_BLOCK + 2 * CONV_HALO, SSD_CONV_DIM), F32)],
        compiler_params=_cparams(("parallel", "parallel")),
        name="ssd_in_proj",
    )(xc, xc, xc, mods, w_main, w_dt, dtb, alog, conv_w, conv_b)


def _scan_block_index(direction, ctx_blocks, n_blocks):
    if direction == 0:
        return lambda j: j
    return lambda j: jnp.where(j < ctx_blocks, ctx_blocks - 1 - j, n_blocks - 1 + ctx_blocks - j)


def _scan_rows(direction, s):
    c = s if direction == 0 else SCAN_SUB - 1 - s
    return pl.ds(pl.multiple_of(c * CHUNK, CHUNK), CHUNK)


def _ssd_scan_kernel(*refs, direction, final):
    if final:
        xs_ref, b_ref, c_ref, dtl_ref, z_ref, yf_ref, dskip_ref, nw_ref, out_ref, h_ref, y_ref = refs
    else:
        xs_ref, b_ref, c_ref, dtl_ref, out_ref, h_ref = refs

    @pl.when(pl.program_id(1) == 0)
    def _():
        h_ref[...] = jnp.zeros(h_ref.shape, F32)

    row = lax.broadcasted_iota(jnp.int32, (CHUNK, CHUNK), 0)
    col = lax.broadcasted_iota(jnp.int32, (CHUNK, CHUNK), 1)
    if direction == 0:
        mask, end = row >= col, CHUNK - 1
    else:
        mask, end = col >= row, 0
    tri = jnp.where(mask, 1.0, 0.0).astype(BF16)
    tri_t = jnp.where(mask, 0.0, 1.0)
    tri_t = jnp.where(row == col, 1.0, tri_t).astype(BF16)
    lane_lo = col < SSD_HEADDIM
    o_dt = SSD_HEADS * direction
    o_la = 2 * SSD_HEADS + SSD_HEADS * direction

    def chunk(s, carry):
        rows = _scan_rows(direction, s)
        dtl = dtl_ref[0, rows, :]
        dtl_t = dtl.T
        p0, p1, p2 = _split3(dtl)
        cum = _dot(tri, p0) + _dot(tri, p1) + _dot(tri, p2)
        q0, q1, q2 = _split3(dtl_t)
        cum_t = _dot(q0, tri_t) + _dot(q1, tri_t) + _dot(q2, tri_t)
        dt_t = dtl_t[o_dt:o_dt + SSD_HEADS, :]
        a_t = cum_t[o_la:o_la + SSD_HEADS, :]
        w1_t = dt_t * jnp.exp(a_t[:, end:end + 1] - a_t)

        for g in range(SSD_GROUPS):
            gs = slice(g * SSD_STATE, (g + 1) * SSD_STATE)
            bg = b_ref[0, rows, gs]
            cg = c_ref[0, rows, gs].astype(BF16)
            cb = lax.dot_general(cg, bg.astype(BF16), (((1,), (1,)), ((), ())), preferred_element_type=F32)
            bg_t = bg.T
            h_in = h_ref[g]
            y_off = _dot(cg, h_in.astype(BF16))
            for jp in range(SSD_HPG // 2):
                ls = slice(g * SSD_GROUP_W + jp * LANES, g * SSD_GROUP_W + (jp + 1) * LANES)
                gl = slice(jp * LANES, (jp + 1) * LANES)
                xp = xs_ref[0, rows, ls]
                top, bot, acols = [], [], []
                for e in range(2):
                    h = g * SSD_HPG + 2 * jp + e
                    a_col = jnp.broadcast_to(cum[:, o_la + h:o_la + h + 1], (CHUNK, CHUNK))
                    seg = jnp.where(mask, a_col - a_t[h:h + 1, :], NEG_BIG)
                    top.append(cb * jnp.exp(seg) * dt_t[h:h + 1, :])
                    bot.append(bg_t * w1_t[h:h + 1, :])
                    acols.append(a_col)
                lhs = jnp.concatenate([jnp.concatenate(top, axis=1), jnp.concatenate(bot, axis=1)], axis=0)
                rhs = jnp.concatenate([jnp.where(lane_lo, xp, 0.0), jnp.where(lane_lo, 0.0, xp)], axis=0)
                res = _dot(lhs.astype(BF16), rhs.astype(BF16))
                ea = jnp.exp(jnp.where(lane_lo, acols[0], acols[1]))
                y_pair = res[0:CHUNK] + y_off[:, gl] * ea
                if final:
                    y_ref[:, ls] = y_pair
                else:
                    out_ref[0, rows, ls] = y_pair
                h_ref[g, :, gl] = h_in[:, gl] * ea[end:end + 1, :] + res[CHUNK:2 * CHUNK]

        if final:
            y = y_ref[...] + yf_ref[0, rows, :] + xs_ref[0, rows, :] * dskip_ref[...]
            y = y * _silu(z_ref[0, rows, :])
            gw = SSD_D_INNER // SSD_GROUPS
            for g in range(SSD_GROUPS):
                gs = slice(g * gw, (g + 1) * gw)
                yg = y[:, gs]
                ms = jnp.mean(yg * yg, axis=-1, keepdims=True)
                out_ref[0, rows, gs] = (yg * lax.rsqrt(ms + LN_EPS) * nw_ref[:, gs]).astype(out_ref.dtype)
        return carry

    lax.fori_loop(0, SCAN_SUB, chunk, 0)


def _ssd_scan(xbc, dtl, direction, ctx_len, z=None, y_f=None, dskip=None, norm_w=None):
    bsz, lt, _ = xbc.shape
    bi = _scan_block_index(direction, ctx_len // ROW_BLOCK, lt // ROW_BLOCK)
    final = z is not None
    col = lambda w, cblk: pl.BlockSpec((1, ROW_BLOCK, w), lambda b, j: (b, bi(j), cblk))
    in_specs = [col(SSD_D_INNER, 0), col(SSD_BC_DIM, SSD_D_INNER // SSD_BC_DIM),
                col(SSD_BC_DIM, SSD_D_INNER // SSD_BC_DIM + 1), col(LANES, 0)]
    args = [xbc, xbc, xbc, dtl]
    scratch = [pltpu.VMEM((SSD_GROUPS, SSD_STATE, SSD_GROUP_W), F32)]
    if final:
        vec = pl.BlockSpec((1, SSD_D_INNER), lambda b, j: (0, 0))
        in_specs += [col(SSD_D_INNER, 0), col(SSD_D_INNER, 0), vec, vec]
        args += [z, y_f, dskip, norm_w]
        scratch.append(pltpu.VMEM((CHUNK, SSD_D_INNER), F32))
    return pl.pallas_call(
        functools.partial(_ssd_scan_kernel, direction=direction, final=final),
        out_shape=jax.ShapeDtypeStruct((bsz, lt, SSD_D_INNER), BF16 if final else F32),
        grid=(bsz, lt // ROW_BLOCK),
        in_specs=in_specs,
        out_specs=col(SSD_D_INNER, 0),
        scratch_shapes=scratch,
        compiler_params=_cparams(("parallel", "arbitrary")),
        name="ssd_scan_bwd" if final else "ssd_scan_fwd",
    )(*args)


def _ret_proj_kernel(x_ref, mod_ref, w_ref, cos_ref, sin_ref, q_ref, k_ref, v_ref, g_ref):
    m = mod_ref[0]
    u = (x_ref[0] * (1.0 + m[1:2, :]) + m[0:1, :]).astype(BF16)
    cs, sn = cos_ref[...], sin_ref[...]
    half = RET_QK_DIM // 2

    def rope(t, out_ref):
        for h in range(RET_HEADS):
            t1 = t[:, h * RET_QK_DIM:h * RET_QK_DIM + half]
            t2 = t[:, h * RET_QK_DIM + half:(h + 1) * RET_QK_DIM]
            out_ref[0, :, h * RET_QK_DIM:h * RET_QK_DIM + half] = t1 * cs - t2 * sn
            out_ref[0, :, h * RET_QK_DIM + half:(h + 1) * RET_QK_DIM] = t1 * sn + t2 * cs

    rope(_dot(u, w_ref[:, 0:D_MODEL]), q_ref)
    rope(_dot(u, w_ref[:, D_MODEL:2 * D_MODEL]) * (RET_QK_DIM ** -0.5), k_ref)
    v_ref[0] = _dot(u, w_ref[:, 2 * D_MODEL:2 * D_MODEL + RET_VALUE])
    g_ref[0] = _dot(u, w_ref[:, 2 * D_MODEL + RET_VALUE:2 * D_MODEL + 2 * RET_VALUE])


def _ret_proj(xc, mods, w, cos, sin, bsz):
    _, lt, d = xc.shape
    row = lambda wd: pl.BlockSpec((1, ROW_BLOCK, wd), lambda b, i: (b, i, 0))
    tab = pl.BlockSpec((ROW_BLOCK, RET_QK_DIM // 2), lambda b, i: (i, 0))
    return pl.pallas_call(
        _ret_proj_kernel,
        out_shape=(jax.ShapeDtypeStruct((bsz, lt, D_MODEL), F32),
                   jax.ShapeDtypeStruct((bsz, lt, D_MODEL), F32),
                   jax.ShapeDtypeStruct((bsz, lt, RET_VALUE), F32),
                   jax.ShapeDtypeStruct((bsz, lt, RET_VALUE), F32)),
        grid=(bsz, lt // ROW_BLOCK),
        in_specs=[row(d), _mod_spec(bsz), pl.BlockSpec(w.shape, lambda b, i: (0, 0)), tab, tab],
        out_specs=(row(D_MODEL), row(D_MODEL), row(RET_VALUE), row(RET_VALUE)),
        compiler_params=_cparams(("parallel", "parallel")),
        name="ret_in_proj",
    )(xc, mods, w, cos, sin)


def _ret_scan_kernel(*refs, direction, final):
    if final:
        (q_ref, k_ref, v_ref, xi_ref, zeta_ref, dmat_ref, cdec_ref,
         g_ref, of_ref, gnw_ref, gnb_ref, out_ref, s_ref) = refs
    else:
        q_ref, k_ref, v_ref, xi_ref, zeta_ref, dmat_ref, cdec_ref, out_ref, s_ref = refs

    @pl.when(pl.program_id(1) == 0)
    def _():
        s_ref[...] = jnp.zeros(s_ref.shape, F32)

    def chunk(s, carry):
        rows = _scan_rows(direction, s)
        for h in range(RET_HEADS):
            qs = slice(h * RET_QK_DIM, (h + 1) * RET_QK_DIM)
            vs = slice(h * RET_V_DIM, (h + 1) * RET_V_DIM)
            qh, kh = q_ref[0, rows, qs], k_ref[0, rows, qs]
            vh = v_ref[0, rows, vs].astype(BF16)
            s_in = s_ref[h]
            sc = lax.dot_general(qh.astype(BF16), kh.astype(BF16), (((1,), (1,)), ((), ())),
                                 preferred_element_type=F32) * dmat_ref[h]
            o = _dot(sc.astype(BF16), vh) + _dot((qh * xi_ref[:, qs]).astype(BF16), s_in.astype(BF16))
            kz_t = (kh * zeta_ref[:, qs]).T
            s_ref[h] = s_in * cdec_ref[h] + _dot(kz_t.astype(BF16), vh)
            if final:
                o = o + of_ref[0, rows, vs]
                mu = jnp.mean(o, axis=-1, keepdims=True)
                var = jnp.mean(jnp.square(o - mu), axis=-1, keepdims=True)
                o = (o - mu) * lax.rsqrt(var + LN_EPS) * gnw_ref[:, vs] + gnb_ref[:, vs]
                out_ref[0, rows, vs] = (o * _silu(g_ref[0, rows, vs])).astype(out_ref.dtype)
            else:
                out_ref[0, rows, vs] = o
        return carry

    lax.fori_loop(0, SCAN_SUB, chunk, 0)


def _ret_scan(q, k, v, tabs, direction, ctx_len, g=None, o_f=None, gn_w=None, gn_b=None):
    bsz, lt, _ = q.shape
    bi = _scan_block_index(direction, ctx_len // ROW_BLOCK, lt // ROW_BLOCK)
    final = g is not None
    xi, zeta, dmat, cdec = tabs
    col = lambda w: pl.BlockSpec((1, ROW_BLOCK, w), lambda b, j: (b, bi(j), 0))
    full = lambda a: pl.BlockSpec(a.shape, lambda b, j: (0,) * a.ndim)
    in_specs = [col(D_MODEL), col(D_MODEL), col(RET_VALUE), full(xi), full(zeta), full(dmat), full(cdec)]
    args = [q, k, v, xi, zeta, dmat, cdec]
    if final:
        vec = pl.BlockSpec((1, RET_VALUE), lambda b, j: (0, 0))
        in_specs += [col(RET_VALUE), col(RET_VALUE), vec, vec]
        args += [g, o_f, gn_w, gn_b]
    return pl.pallas_call(
        functools.partial(_ret_scan_kernel, direction=direction, final=final),
        out_shape=jax.ShapeDtypeStruct((bsz, lt, RET_VALUE), BF16 if final else F32),
        grid=(bsz, lt // ROW_BLOCK),
        in_specs=in_specs,
        out_specs=col(RET_VALUE),
        scratch_shapes=[pltpu.VMEM((RET_HEADS, RET_QK_DIM, RET_V_DIM), F32)],
        compiler_params=_cparams(("parallel", "arbitrary")),
        name="ret_scan_bwd" if final else "ret_scan_fwd",
    )(*args)


def _ret_tables(decay_logit):
    lg = jax.nn.log_sigmoid(decay_logit.astype(F32))
    pos = jnp.arange(CHUNK, dtype=F32)
    rel = pos[:, None] - pos[None, :]
    out = []
    for d in range(2):
        l = lg[d]
        if d == 0:
            xi_e, zeta_e, r = pos + 1.0, CHUNK - 1.0 - pos, rel
        else:
            xi_e, zeta_e, r = CHUNK - pos, pos, -rel
        xi = jnp.repeat(jnp.exp(xi_e[:, None] * l), RET_QK_DIM, axis=1)
        zeta = jnp.repeat(jnp.exp(zeta_e[:, None] * l), RET_QK_DIM, axis=1)
        dmat = jnp.exp(jnp.where((r >= 0)[None], r[None] * l[:, None, None], -jnp.inf))
        cdec = jnp.broadcast_to(jnp.exp(CHUNK * l)[:, None, None], (RET_HEADS, 1, RET_V_DIM))
        out.append((xi, zeta, dmat, cdec))
    return out


def _rope_tables(n_rows, ctx_len):
    rows, cols = jnp.meshgrid(jnp.arange(n_rows), jnp.arange(GRID_W), indexing='ij')
    rows = rows.reshape(-1).astype(F32)
    cols = cols.reshape(-1).astype(F32)
    n_freq = RET_QK_DIM // 4
    inv_freq = ROPE_BASE ** (-jnp.arange(n_freq, dtype=F32) / n_freq)
    ang = jnp.concatenate([rows[:, None] * inv_freq, cols[:, None] * inv_freq], -1)
    cos = jnp.concatenate([jnp.ones((ctx_len, ang.shape[1]), F32), jnp.cos(ang)], 0)
    sin = jnp.concatenate([jnp.zeros((ctx_len, ang.shape[1]), F32), jnp.sin(ang)], 0)
    return cos, sin


def _layer_norm(t, g, b):
    mu = jnp.mean(t, axis=-1, keepdims=True)
    var = jnp.mean(jnp.square(t - mu), axis=-1, keepdims=True)
    return (t - mu) * lax.rsqrt(var + LN_EPS) * g + b


def _out_kernel(y_ref, x_ref, mod_ref, w_ref, lng_ref, lnb_ref, xn_ref, tok_ref):
    m = mod_ref[0]
    o = _dot(y_ref[0], w_ref[...])
    xn = _layer_norm(DEEPNORM_ALPHA * x_ref[0] + m[2:3, :] * o, lng_ref[...], lnb_ref[...])
    xn_ref[0] = xn
    tok_ref[0] = xn * (1.0 + m[4:5, :]) + m[3:4, :]


def _out_proj(y, xc, mods, w, ln_g, ln_b, bsz):
    _, lt, d = xc.shape
    row = lambda wd: pl.BlockSpec((1, ROW_BLOCK, wd), lambda b, i: (b, i, 0))
    vec = pl.BlockSpec((1, d), lambda b, i: (0, 0))
    return pl.pallas_call(
        _out_kernel,
        out_shape=(jax.ShapeDtypeStruct((bsz, lt, d), F32), jax.ShapeDtypeStruct((bsz, lt, d), F32)),
        grid=(bsz, lt // ROW_BLOCK),
        in_specs=[row(y.shape[-1]), row(d), _mod_spec(bsz), pl.BlockSpec(w.shape, lambda b, i: (0, 0)),
                  vec, vec],
        out_specs=(row(d), row(d)),
        compiler_params=_cparams(("parallel", "parallel")),
        name="out_proj_norm",
    )(y, xc, mods, w, ln_g, ln_b)


_META_E, _META_RANK, _META_W = 0, 2, 4


def _router_kernel(tok_ref, w_ref, b_ref, meta_ref, cnt_ref, carry_ref):
    @pl.when(pl.program_id(0) == 0)
    def _():
        carry_ref[...] = jnp.zeros(carry_ref.shape, F32)

    logits = _dot_hi(tok_ref[...], w_ref[...]) + b_ref[...]
    shape = logits.shape
    lane = lax.broadcasted_iota(jnp.int32, shape, 1).astype(F32)
    big = float(LANES)

    def first_max(vals):
        m = jnp.max(vals, axis=-1, keepdims=True)
        return m, jnp.min(jnp.where(vals == m, lane, big), axis=-1, keepdims=True)

    gl = jnp.where(lane < MOE_GROUPS, logits, -jnp.inf)
    gmax, gsel = first_max(gl)
    g_gate = 1.0 / jnp.sum(jnp.exp(gl - gmax), axis=-1, keepdims=True)
    lo = MOE_GROUPS + gsel * MOE_EXPERTS_PER_GROUP
    el = jnp.where((lane >= lo) & (lane < lo + MOE_EXPERTS_PER_GROUP), logits, -jnp.inf)
    m1, i1 = first_max(el)
    m2, i2 = first_max(jnp.where(lane == i1, -jnp.inf, el))
    e2 = jnp.exp(m2 - m1)
    w1 = g_gate / (1.0 + e2)
    w2 = g_gate * e2 / (1.0 + e2)
    e1, e2id = i1 - MOE_GROUPS, i2 - MOE_GROUPS

    oh1 = jnp.where(lane == e1, 1.0, 0.0)
    oh2 = jnp.where(lane == e2id, 1.0, 0.0)
    ohs = oh1 + oh2
    n = shape[0]
    r = lax.broadcasted_iota(jnp.int32, (n, n), 0)
    c = lax.broadcasted_iota(jnp.int32, (n, n), 1)
    before = _dot(jnp.where(c < r, 1.0, 0.0).astype(BF16), ohs.astype(BF16)) + carry_ref[...]
    rank1 = jnp.sum(oh1 * before, axis=-1, keepdims=True)
    rank2 = jnp.sum(oh2 * before, axis=-1, keepdims=True)
    carry_ref[...] = carry_ref[...] + jnp.sum(ohs, axis=0, keepdims=True)
    cnt_ref[...] = carry_ref[...]

    rec = jnp.zeros(shape, F32)
    for k, val in enumerate((e1, e2id, rank1, rank2, w1, w2)):
        rec = jnp.where(lane == float(k), val, rec)
    meta_ref[...] = rec


def _router(tokens, w_r, b_r):
    t, d = tokens.shape
    return pl.pallas_call(
        _router_kernel,
        out_shape=(jax.ShapeDtypeStruct((t, LANES), F32), jax.ShapeDtypeStruct((1, LANES), F32)),
        grid=(t // ROW_BLOCK,),
        in_specs=[pl.BlockSpec((ROW_BLOCK, d), lambda i: (i, 0)),
                  pl.BlockSpec((d, LANES), lambda i: (0, 0)),
                  pl.BlockSpec((1, LANES), lambda i: (0, 0))],
        out_specs=(pl.BlockSpec((ROW_BLOCK, LANES), lambda i: (i, 0)),
                   pl.BlockSpec((1, LANES), lambda i: (0, 0))),
        scratch_shapes=[pltpu.VMEM((1, LANES), F32)],
        compiler_params=_cparams(("arbitrary",)),
        name="moe_router",
    )(tokens, w_r, b_r)


def _row_copy(src_ref, src_row, dst_ref, dst_row, sem):
    return pltpu.make_async_copy(src_ref.at[pl.ds(src_row, 1), :], dst_ref.at[pl.ds(dst_row, 1), :], sem)


def _for_block_rows(fn):
    def body(rb, carry):
        for u in range(ROW_DMA_UNROLL):
            for k in range(MOE_TOP_K):
                fn(rb * ROW_DMA_UNROLL + u, k)
        return carry
    lax.fori_loop(0, ROW_BLOCK // ROW_DMA_UNROLL, body, 0)


def _dispatch_kernel(dest_ref, tok_ref, buf_in_ref, buf_ref, stage_ref, load_sem, row_sem):
    del buf_in_ref
    i = pl.program_id(0)
    n = pl.num_programs(0)

    def load(blk):
        slot = blk % 3
        return pltpu.make_async_copy(tok_ref.at[pl.ds(blk * ROW_BLOCK, ROW_BLOCK), :], stage_ref.at[slot],
                                     load_sem.at[slot])

    @pl.when(i == 0)
    def _():
        load(i).start()

    @pl.when(i + 1 < n)
    def _():
        load(i + 1).start()

    load(i).wait()
    stage = stage_ref.at[i % 3]
    base = i * ROW_BLOCK
    _for_block_rows(lambda r, k: _row_copy(stage, r, buf_ref, dest_ref[(base + r) * MOE_TOP_K + k],
                                           row_sem.at[i % 2]).start(priority=k))

    def drain(blk):
        _for_block_rows(lambda r, k: _row_copy(stage_ref.at[0], 0, buf_ref, 0, row_sem.at[blk % 2]).wait())

    @pl.when(i > 0)
    def _():
        drain(i - 1)

    @pl.when(i == n - 1)
    def _():
        drain(i)


def _dispatch(dest_flat, tokens, buf_prev):
    t, d = tokens.shape
    return pl.pallas_call(
        _dispatch_kernel,
        out_shape=jax.ShapeDtypeStruct(buf_prev.shape, F32),
        grid_spec=pltpu.PrefetchScalarGridSpec(
            num_scalar_prefetch=1,
            grid=(t // ROW_BLOCK,),
            in_specs=[pl.BlockSpec(memory_space=pl.ANY), pl.BlockSpec(memory_space=pl.ANY)],
            out_specs=pl.BlockSpec(memory_space=pl.ANY),
            scratch_shapes=[pltpu.VMEM((3, ROW_BLOCK, d), F32), pltpu.SemaphoreType.DMA((3,)),
                            pltpu.SemaphoreType.DMA((2,))],
        ),
        input_output_aliases={2: 0},
        compiler_params=_cparams(("arbitrary",)),
        name="moe_dispatch",
    )(dest_flat, tokens, buf_prev)


def _expert_kernel(be_ref, nused_ref, x_ref, wgu_ref, wd_ref, o_ref, wgu_bf_ref, wd_bf_ref):
    i = pl.program_id(0)

    @pl.when((i == 0) | (be_ref[i] != be_ref[jnp.maximum(i - 1, 0)]))
    def _():
        wgu_bf_ref[...] = wgu_ref[...].astype(BF16)
        wd_bf_ref[...] = wd_ref[...].astype(BF16)

    @pl.when(i < nused_ref[0])
    def _():
        gu = _dot(x_ref[...].astype(BF16), wgu_bf_ref[...])
        act = _silu(gu[:, :MOE_HIDDEN]) * gu[:, MOE_HIDDEN:]
        o_ref[...] = _dot(act.astype(BF16), wd_bf_ref[...])

    @pl.when(i >= nused_ref[0])
    def _():
        o_ref[...] = jnp.zeros(o_ref.shape, F32)


def _experts(block_e, n_used, buf, w_gu, w_d, layer):
    n_rows, d = buf.shape
    return pl.pallas_call(
        _expert_kernel,
        out_shape=jax.ShapeDtypeStruct((n_rows, d), F32),
        grid_spec=pltpu.PrefetchScalarGridSpec(
            num_scalar_prefetch=2,
            grid=(n_rows // ROW_BLOCK,),
            in_specs=[pl.BlockSpec((ROW_BLOCK, d), lambda i, be, nu: (i, 0)),
                      pl.BlockSpec((None, None, d, 2 * MOE_HIDDEN), lambda i, be, nu: (layer, be[i], 0, 0)),
                      pl.BlockSpec((None, None, MOE_HIDDEN, d), lambda i, be, nu: (layer, be[i], 0, 0))],
            out_specs=pl.BlockSpec((ROW_BLOCK, d), lambda i, be, nu: (i, 0)),
            scratch_shapes=[pltpu.VMEM((d, 2 * MOE_HIDDEN), BF16), pltpu.VMEM((MOE_HIDDEN, d), BF16)],
        ),
        compiler_params=_cparams(("arbitrary",)),
        name="moe_experts",
    )(block_e, n_used, buf, w_gu, w_d)


def _combine_kernel(dest_ref, eo_ref, meta_ref, x_ref, mod_ref, lng_ref, lnb_ref, out_ref, g_ref, sem):
    nblk = pl.num_programs(1)
    f = pl.program_id(0) * nblk + pl.program_id(1)
    n = pl.num_programs(0) * nblk

    def gather(blk):
        slot = blk % 2
        base = blk * ROW_BLOCK
        _for_block_rows(lambda r, k: _row_copy(eo_ref, dest_ref[(base + r) * MOE_TOP_K + k], g_ref.at[slot, k],
                                               r, sem.at[slot]).start(priority=k))

    @pl.when(f == 0)
    def _():
        gather(f)

    @pl.when(f + 1 < n)
    def _():
        gather(f + 1)

    slot = f % 2
    _for_block_rows(lambda r, k: _row_copy(eo_ref, 0, g_ref.at[slot, k], 0, sem.at[slot]).wait())

    meta = meta_ref[...]
    ffn = g_ref[slot, 0] * meta[:, _META_W:_META_W + 1] + g_ref[slot, 1] * meta[:, _META_W + 1:_META_W + 2]
    m = mod_ref[0]
    out_ref[0] = _layer_norm(DEEPNORM_ALPHA * x_ref[0] + m[5:6, :] * ffn, lng_ref[...], lnb_ref[...])


def _combine(dest_flat, eo, meta, xn, mods, ln_g, ln_b, bsz):
    _, lt, d = xn.shape
    nblk = lt // ROW_BLOCK
    return pl.pallas_call(
        _combine_kernel,
        out_shape=jax.ShapeDtypeStruct((bsz, lt, d), F32),
        grid_spec=pltpu.PrefetchScalarGridSpec(
            num_scalar_prefetch=1,
            grid=(bsz, nblk),
            in_specs=[pl.BlockSpec(memory_space=pl.ANY),
                      pl.BlockSpec((ROW_BLOCK, LANES), lambda b, i, dest: (b * nblk + i, 0)),
                      pl.BlockSpec((1, ROW_BLOCK, d), lambda b, i, dest: (b, i, 0)),
                      pl.BlockSpec((1, 6, d), lambda b, i, dest: (jnp.where(i == 0, bsz, b), 0, 0)),
                      pl.BlockSpec((1, d), lambda b, i, dest: (0, 0)),
                      pl.BlockSpec((1, d), lambda b, i, dest: (0, 0))],
            out_specs=pl.BlockSpec((1, ROW_BLOCK, d), lambda b, i, dest: (b, i, 0)),
            scratch_shapes=[pltpu.VMEM((2, MOE_TOP_K, ROW_BLOCK, d), F32), pltpu.SemaphoreType.DMA((2,))],
        ),
        compiler_params=_cparams(("arbitrary", "arbitrary")),
        name="moe_combine",
    )(dest_flat, eo, meta, xn, mods, ln_g, ln_b)


def _moe_rows(n_tokens):
    return (MOE_TOP_K * n_tokens + MOE_EXPERTS * (ROW_BLOCK - 1)) // ROW_BLOCK * ROW_BLOCK


def _moe(tokens3, xn, mods, w_r, b_r, w_gu, w_d, layer, ln_g, ln_b, bsz, buf_prev):
    _, lt, d = tokens3.shape
    tokens = tokens3.reshape(bsz * lt, d)
    n_blocks = buf_prev.shape[0] // ROW_BLOCK
    meta, cnt = _router(tokens, w_r, b_r)
    eid = meta[:, _META_E:_META_E + MOE_TOP_K].astype(jnp.int32)
    rank = meta[:, _META_RANK:_META_RANK + MOE_TOP_K].astype(jnp.int32)
    counts = cnt[0, :MOE_EXPERTS].astype(jnp.int32)
    padded = (counts + ROW_BLOCK - 1) // ROW_BLOCK * ROW_BLOCK
    pends = jnp.cumsum(padded)
    dest_flat = (jnp.take(pends - padded, eid) + rank).reshape(-1)
    block_start = jnp.arange(n_blocks, dtype=jnp.int32) * ROW_BLOCK
    block_e = jnp.minimum(jnp.sum((pends[None, :] <= block_start[:, None]).astype(jnp.int32), axis=1),
                          MOE_EXPERTS - 1)
    n_used = (pends[-1:] // ROW_BLOCK).astype(jnp.int32)
    buf = _dispatch(dest_flat, tokens, buf_prev)
    eo = _experts(block_e, n_used, buf, w_gu, w_d, layer)
    return _combine(dest_flat, eo, meta, xn, mods, ln_g, ln_b, bsz), buf


def kernel(x, c, ctx, c_ctx, mod_w, mod_b, ssd_in_w, ssd_conv_w, ssd_conv_b, ssd_dt_bias, ssd_a_log, ssd_d_skip, ssd_norm_w, ssd_out_w, ret_in_w, ret_decay_logit, ret_gn_w, ret_gn_b, ret_out_w, ln_mix_g, ln_mix_b, ln_ffn_g, ln_ffn_b, moe_group_w, moe_group_b, moe_expert_w, moe_expert_b, moe_w_gate_up, moe_w_down):
    bsz, seqlen, d = x.shape
    ctx_len = ctx.shape[1]
    assert d == D_MODEL and ctx_len == ROW_BLOCK and seqlen % ROW_BLOCK == 0 and seqlen % GRID_W == 0
    assert bsz + 1 <= 16
    depth = mod_w.shape[0]

    xc = jnp.concatenate([ctx, x], axis=1)
    cc = jnp.zeros((16, d), F32).at[:bsz].set(c).at[bsz].set(c_ctx)
    cos, sin = _rope_tables(seqlen // GRID_W, ctx_len)
    w_gu, w_d = moe_w_gate_up, moe_w_down
    buf = jnp.zeros((_moe_rows(bsz * (ctx_len + seqlen)), d), F32)

    for i in range(depth):
        j = i // 2
        mods = _modulation(cc, mod_w, mod_b, i)
        if i % 2 == 0:
            w = ssd_in_w[j]
            n_main = SSD_D_INNER + SSD_CONV_DIM
            w_main = w[:, :n_main].astype(BF16)
            w_dt = jnp.concatenate([w[:, n_main:], w[:, n_main:]], axis=1).astype(BF16)
            dtb = jnp.tile(ssd_dt_bias[j].reshape(1, -1), (1, 2))
            alog = jnp.tile(ssd_a_log[j].reshape(1, -1), (1, 2))
            z, xbc, dtl = _ssd_proj(xc, mods, w_main, w_dt, dtb, alog, ssd_conv_w[j],
                                    ssd_conv_b[j].reshape(1, -1), bsz)
            y_f = _ssd_scan(xbc, dtl, 0, ctx_len)
            dskip = jnp.repeat(ssd_d_skip[j], SSD_HEADDIM).reshape(1, -1)
            y = _ssd_scan(xbc, dtl, 1, ctx_len, z=z, y_f=y_f, dskip=dskip,
                          norm_w=ssd_norm_w[j].reshape(1, -1))
            out_w = ssd_out_w[j]
        else:
            q, k, v, g = _ret_proj(xc, mods, ret_in_w[j].astype(BF16), cos, sin, bsz)
            tabs = _ret_tables(ret_decay_logit[j])
            o_f = _ret_scan(q, k, v, tabs[0], 0, ctx_len)
            y = _ret_scan(q, k, v, tabs[1], 1, ctx_len, g=g, o_f=o_f,
                          gn_w=ret_gn_w[j].reshape(1, -1), gn_b=ret_gn_b[j].reshape(1, -1))
            out_w = ret_out_w[j]
        xn, tokens = _out_proj(y, xc, mods, out_w.astype(BF16), ln_mix_g[i].reshape(1, -1),
                               ln_mix_b[i].reshape(1, -1), bsz)
        w_r = jnp.zeros((d, LANES), F32).at[:, :MOE_GROUPS].set(moe_group_w[i])
        w_r = w_r.at[:, MOE_GROUPS:MOE_GROUPS + MOE_EXPERTS].set(moe_expert_w[i])
        b_r = jnp.zeros((1, LANES), F32).at[0, :MOE_GROUPS].set(moe_group_b[i])
        b_r = b_r.at[0, MOE_GROUPS:MOE_GROUPS + MOE_EXPERTS].set(moe_expert_b[i])
        xc, buf = _moe(tokens, xn, mods, w_r, b_r, w_gu, w_d, i, ln_ffn_g[i].reshape(1, -1),
                       ln_ffn_b[i].reshape(1, -1), bsz, buf)
    return xc[:, ctx_len:, :]
```

```python
import functools

import jax
import jax.numpy as jnp
from jax import lax
from jax.experimental import pallas as pl
from jax.experimental.pallas import tpu as pltpu

F32 = jnp.float32
BF16 = jnp.bfloat16

D_MODEL = 1024
DEPTH = 4
GRID_W = 64
DEEPNORM_ALPHA = (2.0 * DEPTH) ** 0.25
LN_EPS = 1e-5

SSD_D_INNER = 2 * D_MODEL
SSD_HEADDIM = 64
SSD_HEADS = SSD_D_INNER // SSD_HEADDIM
SSD_GROUPS = 4
SSD_HPG = SSD_HEADS // SSD_GROUPS
SSD_STATE = 128
SSD_CONV_W = 5
SSD_BC_DIM = SSD_GROUPS * SSD_STATE
SSD_CONV_DIM = SSD_D_INNER + 2 * SSD_BC_DIM
SSD_GROUP_W = SSD_HPG * SSD_HEADDIM

RET_HEADS = D_MODEL // 256
RET_QK_DIM = D_MODEL // RET_HEADS
RET_VALUE = 2 * D_MODEL
RET_V_DIM = RET_VALUE // RET_HEADS
ROPE_BASE = 10000.0

MOE_GROUPS = 4
MOE_EXPERTS_PER_GROUP = 8
MOE_EXPERTS = MOE_GROUPS * MOE_EXPERTS_PER_GROUP
MOE_HIDDEN = D_MODEL // 2
MOE_TOP_K = 2

CHUNK = 128
LANES = 128
ROW_BLOCK = 256
EXPERT_BLOCK = 512
SCAN_SUB = ROW_BLOCK // CHUNK
CONV_HALO = 8
CONV_COLS = 256
ROW_DMA_UNROLL = 8
VMEM_LIMIT = 48 * 1024 * 1024
NEG_BIG = -1e30


def _cparams(sem):
    return pltpu.CompilerParams(dimension_semantics=sem, vmem_limit_bytes=VMEM_LIMIT)


def _dot(a, b):
    return jnp.dot(a, b, preferred_element_type=F32)


def _split2(x):
    hi = x.astype(BF16)
    lo = (x - hi.astype(F32)).astype(BF16)
    return hi, lo


def _split3(x):
    hi = x.astype(BF16)
    r = x - hi.astype(F32)
    mid = r.astype(BF16)
    lo = (r - mid.astype(F32)).astype(BF16)
    return hi, mid, lo


def _dot_hi(a, b):
    ah, al = _split2(a)
    bh, bl = _split2(b)
    return _dot(ah, bh) + (_dot(ah, bl) + _dot(al, bh))


def _silu(x):
    return x * jax.nn.sigmoid(x)


def _softplus(x):
    return jnp.maximum(x, 0.0) + jnp.log1p(jnp.exp(-jnp.abs(x)))


def _mod_kernel(c_ref, w_ref, b_ref, o_ref):
    o_ref[...] = _dot_hi(_silu(c_ref[...]), w_ref[...]) + b_ref[...]


def _modulation(cc, mod_w, mod_b, layer):
    rows, d = cc.shape
    n = mod_w.shape[-1]
    tn = 1024
    out = pl.pallas_call(
        _mod_kernel,
        out_shape=jax.ShapeDtypeStruct((rows, n), F32),
        grid=(n // tn,),
        in_specs=[pl.BlockSpec((rows, d), lambda j: (0, 0)),
                  pl.BlockSpec((None, d, tn), lambda j: (layer, 0, j)),
                  pl.BlockSpec((None, 1, tn), lambda j: (layer, 0, j))],
        out_specs=pl.BlockSpec((rows, tn), lambda j: (0, j)),
        compiler_params=_cparams(("parallel",)),
        name="modulation",
    )(cc, mod_w, mod_b.reshape(mod_b.shape[0], 1, n))
    return out.reshape(rows, 6, d)


def _mod_spec(bsz):
    return pl.BlockSpec((1, 6, D_MODEL), lambda b, i: (jnp.where(i == 0, bsz, b), 0, 0))


def _ssd_proj_kernel(x_ref, xlo_ref, xhi_ref, mod_ref, w_ref, wdt_ref, dtb_ref, alog_ref, cw_ref, cb_ref,
                     z_ref, xbc_ref, dtl_ref):
    i = pl.program_id(1)
    m = mod_ref[0]
    main = slice(CONV_HALO, CONV_HALO + ROW_BLOCK)
    n_all = ROW_BLOCK + 2 * CONV_HALO
    x_all = jnp.concatenate([xlo_ref[0], x_ref[0], xhi_ref[0]], axis=0)
    u = (x_all * (1.0 + m[1:2, :]) + m[0:1, :]).astype(BF16)
    r = lax.broadcasted_iota(jnp.int32, (n_all, 1), 0)
    lo_ok = jnp.where(i >= 2, 1.0, 0.0)
    hi_ok = jnp.where((i >= 1) & (i < pl.num_programs(1) - 1), 1.0, 0.0)
    keep = jnp.where(r < CONV_HALO, lo_ok, jnp.where(r >= CONV_HALO + ROW_BLOCK, hi_ok, 1.0))
    half = SSD_CONV_W // 2
    for c0 in range(0, SSD_CONV_DIM, CONV_COLS):
        cs = slice(c0, c0 + CONV_COLS)
        v = _dot(u, w_ref[:, SSD_D_INNER + c0:SSD_D_INNER + c0 + CONV_COLS]) * keep
        if c0 < SSD_D_INNER:
            z_ref[0, :, cs] = _dot(u, w_ref[:, cs])[main]
        acc = cb_ref[:, cs] + cw_ref[half:half + 1, cs] * v[main]
        for k in range(SSD_CONV_W):
            if k != half:
                acc = acc + cw_ref[k:k + 1, cs] * pltpu.roll(v, (half - k) % n_all, 0)[main]
        xbc_ref[0, :, cs] = _silu(acc)
    dt = _softplus(_dot(u, wdt_ref[...])[main] + dtb_ref[...])
    lane = lax.broadcasted_iota(jnp.int32, dt.shape, 1)
    dtl_ref[0] = jnp.where(lane < 2 * SSD_HEADS, dt, dt * (-jnp.exp(alog_ref[...])))


def _ssd_proj(xc, mods, w_main, w_dt, dtb, alog, conv_w, conv_b, bsz):
    _, lt, d = xc.shape
    nblk = lt // ROW_BLOCK
    per = ROW_BLOCK // CONV_HALO
    row = lambda w: pl.BlockSpec((1, ROW_BLOCK, w), lambda b, i: (b, i, 0))
    full = lambda a: pl.BlockSpec(a.shape, lambda b, i: (0,) * a.ndim)
    lo = pl.BlockSpec((1, CONV_HALO, d), lambda b, i: (b, jnp.maximum(i * per - 1, 0), 0))
    hi = pl.BlockSpec((1, CONV_HALO, d), lambda b, i: (b, jnp.minimum((i + 1) * per, nblk * per - 1), 0))
    return pl.pallas_call(
        _ssd_proj_kernel,
        out_shape=(jax.ShapeDtypeStruct((bsz, lt, SSD_D_INNER), F32),
                   jax.ShapeDtypeStruct((bsz, lt, SSD_CONV_DIM), F32),
                   jax.ShapeDtypeStruct((bsz, lt, LANES), F32)),
        grid=(bsz, nblk),
        in_specs=[row(d), lo, hi, _mod_spec(bsz), full(w_main), full(w_dt), full(dtb), full(alog),
                  full(conv_w), full(conv_b)],
        out_specs=(row(SSD_D_INNER), row(SSD_CONV_DIM), row(LANES)),
        compiler_params=_cparams(("parallel", "parallel")),
        name="ssd_in_proj",
    )(xc, xc, xc, mods, w_main, w_dt, dtb, alog, conv_w, conv_b)


def _scan_block_index(direction, ctx_blocks, n_blocks):
    if direction == 0:
        return lambda j: j
    return lambda j: jnp.where(j < ctx_blocks, ctx_blocks - 1 - j, n_blocks - 1 + ctx_blocks - j)


def _scan_rows(direction, s):
    c = s if direction == 0 else SCAN_SUB - 1 - s
    return slice(c * CHUNK, (c + 1) * CHUNK)


def _ssd_scan_kernel(*refs, direction, final):
    if final:
        xs_ref, b_ref, c_ref, dtl_ref, z_ref, yf_ref, dskip_ref, nw_ref, out_ref, h_ref, y_ref = refs
    else:
        xs_ref, b_ref, c_ref, dtl_ref, out_ref, h_ref = refs

    @pl.when(pl.program_id(1) == 0)
    def _():
        h_ref[...] = jnp.zeros(h_ref.shape, F32)

    row = lax.broadcasted_iota(jnp.int32, (CHUNK, CHUNK), 0)
    col = lax.broadcasted_iota(jnp.int32, (CHUNK, CHUNK), 1)
    if direction == 0:
        mask, end = row >= col, CHUNK - 1
    else:
        mask, end = col >= row, 0
    tri = jnp.where(mask, 1.0, 0.0).astype(BF16)
    tri_t = jnp.where(mask, 0.0, 1.0)
    tri_t = jnp.where(row == col, 1.0, tri_t).astype(BF16)
    lane_lo = col < SSD_HEADDIM
    o_dt = SSD_HEADS * direction
    o_la = 2 * SSD_HEADS + SSD_HEADS * direction

    for s in range(SCAN_SUB):
        rows = _scan_rows(direction, s)
        dtl = dtl_ref[0, rows, :]
        dtl_t = dtl.T
        p0, p1, p2 = _split3(dtl)
        cum = _dot(tri, p0) + _dot(tri, p1) + _dot(tri, p2)
        q0, q1, q2 = _split3(dtl_t)
        cum_t = _dot(q0, tri_t) + _dot(q1, tri_t) + _dot(q2, tri_t)
        dt_t = dtl_t[o_dt:o_dt + SSD_HEADS, :]
        a_t = cum_t[o_la:o_la + SSD_HEADS, :]
        w1_t = dt_t * jnp.exp(a_t[:, end:end + 1] - a_t)
        a_dt_t = a_t - jnp.log(dt_t)

        for g in range(SSD_GROUPS):
            gs = slice(g * SSD_STATE, (g + 1) * SSD_STATE)
            bg = b_ref[0, rows, gs]
            cg = c_ref[0, rows, gs].astype(BF16)
            cb = lax.dot_general(cg, bg.astype(BF16), (((1,), (1,)), ((), ())), preferred_element_type=F32)
            bg_t = bg.T
            h_in = h_ref[g]
            y_off = _dot(cg, h_in.astype(BF16))
            for jp in range(SSD_HPG // 2):
                ls = slice(g * SSD_GROUP_W + jp * LANES, g * SSD_GROUP_W + (jp + 1) * LANES)
                gl = slice(jp * LANES, (jp + 1) * LANES)
                xp = xs_ref[0, rows, ls]
                top, bot, acols = [], [], []
                for e in range(2):
                    h = g * SSD_HPG + 2 * jp + e
                    a_col = jnp.broadcast_to(cum[:, o_la + h:o_la + h + 1], (CHUNK, CHUNK))
                    seg = jnp.where(mask, a_col - a_dt_t[h:h + 1, :], NEG_BIG)
                    top.append(cb * jnp.exp(seg))
                    bot.append(bg_t * w1_t[h:h + 1, :])
                    acols.append(a_col)
                lhs = jnp.concatenate([jnp.concatenate(top, axis=1), jnp.concatenate(bot, axis=1)], axis=0)
                rhs = jnp.concatenate([jnp.where(lane_lo, xp, 0.0), jnp.where(lane_lo, 0.0, xp)], axis=0)
                res = _dot(lhs.astype(BF16), rhs.astype(BF16))
                ea = jnp.exp(jnp.where(lane_lo, acols[0], acols[1]))
                y_pair = res[0:CHUNK] + y_off[:, gl] * ea
                if final:
                    y_ref[:, ls] = y_pair
                else:
                    out_ref[0, rows, ls] = y_pair
                h_ref[g, :, gl] = h_in[:, gl] * ea[end:end + 1, :] + res[CHUNK:2 * CHUNK]

        if final:
            y = y_ref[...] + yf_ref[0, rows, :] + xs_ref[0, rows, :] * dskip_ref[...]
            y = y * _silu(z_ref[0, rows, :])
            gw = SSD_D_INNER // SSD_GROUPS
            for g in range(SSD_GROUPS):
                gs = slice(g * gw, (g + 1) * gw)
                yg = y[:, gs]
                ms = jnp.mean(yg * yg, axis=-1, keepdims=True)
                out_ref[0, rows, gs] = (yg * lax.rsqrt(ms + LN_EPS) * nw_ref[:, gs]).astype(out_ref.dtype)


def _ssd_scan(xbc, dtl, direction, ctx_len, z=None, y_f=None, dskip=None, norm_w=None):
    bsz, lt, _ = xbc.shape
    bi = _scan_block_index(direction, ctx_len // ROW_BLOCK, lt // ROW_BLOCK)
    final = z is not None
    col = lambda w, cblk: pl.BlockSpec((1, ROW_BLOCK, w), lambda b, j: (b, bi(j), cblk))
    in_specs = [col(SSD_D_INNER, 0), col(SSD_BC_DIM, SSD_D_INNER // SSD_BC_DIM),
                col(SSD_BC_DIM, SSD_D_INNER // SSD_BC_DIM + 1), col(LANES, 0)]
    args = [xbc, xbc, xbc, dtl]
    scratch = [pltpu.VMEM((SSD_GROUPS, SSD_STATE, SSD_GROUP_W), F32)]
    if final:
        vec = pl.BlockSpec((1, SSD_D_INNER), lambda b, j: (0, 0))
        in_specs += [col(SSD_D_INNER, 0), col(SSD_D_INNER, 0), vec, vec]
        args += [z, y_f, dskip, norm_w]
        scratch.append(pltpu.VMEM((CHUNK, SSD_D_INNER), F32))
    return pl.pallas_call(
        functools.partial(_ssd_scan_kernel, direction=direction, final=final),
        out_shape=jax.ShapeDtypeStruct((bsz, lt, SSD_D_INNER), BF16 if final else F32),
        grid=(bsz, lt // ROW_BLOCK),
        in_specs=in_specs,
        out_specs=col(SSD_D_INNER, 0),
        scratch_shapes=scratch,
        compiler_params=_cparams(("parallel", "arbitrary")),
        name="ssd_scan_bwd" if final else "ssd_scan_fwd",
    )(*args)


def _ret_proj_kernel(x_ref, mod_ref, w_ref, cos_ref, sin_ref, q_ref, k_ref, v_ref, g_ref):
    m = mod_ref[0]
    u = (x_ref[0] * (1.0 + m[1:2, :]) + m[0:1, :]).astype(BF16)
    cs, sn = cos_ref[...], sin_ref[...]
    half = RET_QK_DIM // 2

    def rope(t, out_ref):
        for h in range(RET_HEADS):
            t1 = t[:, h * RET_QK_DIM:h * RET_QK_DIM + half]
            t2 = t[:, h * RET_QK_DIM + half:(h + 1) * RET_QK_DIM]
            out_ref[0, :, h * RET_QK_DIM:h * RET_QK_DIM + half] = t1 * cs - t2 * sn
            out_ref[0, :, h * RET_QK_DIM + half:(h + 1) * RET_QK_DIM] = t1 * sn + t2 * cs

    rope(_dot(u, w_ref[:, 0:D_MODEL]), q_ref)
    rope(_dot(u, w_ref[:, D_MODEL:2 * D_MODEL]) * (RET_QK_DIM ** -0.5), k_ref)
    v_ref[0] = _dot(u, w_ref[:, 2 * D_MODEL:2 * D_MODEL + RET_VALUE])
    g_ref[0] = _dot(u, w_ref[:, 2 * D_MODEL + RET_VALUE:2 * D_MODEL + 2 * RET_VALUE])


def _ret_proj(xc, mods, w, cos, sin, bsz):
    _, lt, d = xc.shape
    row = lambda wd: pl.BlockSpec((1, ROW_BLOCK, wd), lambda b, i: (b, i, 0))
    tab = pl.BlockSpec((ROW_BLOCK, RET_QK_DIM // 2), lambda b, i: (i, 0))
    return pl.pallas_call(
        _ret_proj_kernel,
        out_shape=(jax.ShapeDtypeStruct((bsz, lt, D_MODEL), F32),
                   jax.ShapeDtypeStruct((bsz, lt, D_MODEL), F32),
                   jax.ShapeDtypeStruct((bsz, lt, RET_VALUE), F32),
                   jax.ShapeDtypeStruct((bsz, lt, RET_VALUE), F32)),
        grid=(bsz, lt // ROW_BLOCK),
        in_specs=[row(d), _mod_spec(bsz), pl.BlockSpec(w.shape, lambda b, i: (0, 0)), tab, tab],
        out_specs=(row(D_MODEL), row(D_MODEL), row(RET_VALUE), row(RET_VALUE)),
        compiler_params=_cparams(("parallel", "parallel")),
        name="ret_in_proj",
    )(xc, mods, w, cos, sin)


def _ret_scan_kernel(*refs, direction, final):
    if final:
        (q_ref, k_ref, v_ref, xi_ref, zeta_ref, dmat_ref, cdec_ref,
         g_ref, of_ref, gnw_ref, gnb_ref, out_ref, s_ref) = refs
    else:
        q_ref, k_ref, v_ref, xi_ref, zeta_ref, dmat_ref, cdec_ref, out_ref, s_ref = refs

    @pl.when(pl.program_id(1) == 0)
    def _():
        s_ref[...] = jnp.zeros(s_ref.shape, F32)

    for s in range(SCAN_SUB):
        rows = _scan_rows(direction, s)
        for h in range(RET_HEADS):
            qs = slice(h * RET_QK_DIM, (h + 1) * RET_QK_DIM)
            vs = slice(h * RET_V_DIM, (h + 1) * RET_V_DIM)
            qh, kh = q_ref[0, rows, qs], k_ref[0, rows, qs]
            vh = v_ref[0, rows, vs].astype(BF16)
            s_in = s_ref[h]
            sc = lax.dot_general(qh.astype(BF16), kh.astype(BF16), (((1,), (1,)), ((), ())),
                                 preferred_element_type=F32) * dmat_ref[h]
            o = _dot(sc.astype(BF16), vh) + _dot((qh * xi_ref[:, qs]).astype(BF16), s_in.astype(BF16))
            kz_t = (kh * zeta_ref[:, qs]).T
            s_ref[h] = s_in * cdec_ref[h] + _dot(kz_t.astype(BF16), vh)
            if final:
                o = o + of_ref[0, rows, vs]
                mu = jnp.mean(o, axis=-1, keepdims=True)
                var = jnp.mean(jnp.square(o - mu), axis=-1, keepdims=True)
                o = (o - mu) * lax.rsqrt(var + LN_EPS) * gnw_ref[:, vs] + gnb_ref[:, vs]
                out_ref[0, rows, vs] = (o * _silu(g_ref[0, rows, vs])).astype(out_ref.dtype)
            else:
                out_ref[0, rows, vs] = o


def _ret_scan(q, k, v, tabs, direction, ctx_len, g=None, o_f=None, gn_w=None, gn_b=None):
    bsz, lt, _ = q.shape
    bi = _scan_block_index(direction, ctx_len // ROW_BLOCK, lt // ROW_BLOCK)
    final = g is not None
    xi, zeta, dmat, cdec = tabs
    col = lambda w: pl.BlockSpec((1, ROW_BLOCK, w), lambda b, j: (b, bi(j), 0))
    full = lambda a: pl.BlockSpec(a.shape, lambda b, j: (0,) * a.ndim)
    in_specs = [col(D_MODEL), col(D_MODEL), col(RET_VALUE), full(xi), full(zeta), full(dmat), full(cdec)]
    args = [q, k, v, xi, zeta, dmat, cdec]
    if final:
        vec = pl.BlockSpec((1, RET_VALUE), lambda b, j: (0, 0))
        in_specs += [col(RET_VALUE), col(RET_VALUE), vec, vec]
        args += [g, o_f, gn_w, gn_b]
    return pl.pallas_call(
        functools.partial(_ret_scan_kernel, direction=direction, final=final),
        out_shape=jax.ShapeDtypeStruct((bsz, lt, RET_VALUE), BF16 if final else F32),
        grid=(bsz, lt // ROW_BLOCK),
        in_specs=in_specs,
        out_specs=col(RET_VALUE),
        scratch_shapes=[pltpu.VMEM((RET_HEADS, RET_QK_DIM, RET_V_DIM), F32)],
        compiler_params=_cparams(("parallel", "arbitrary")),
        name="ret_scan_bwd" if final else "ret_scan_fwd",
    )(*args)


def _ret_tables(decay_logit):
    lg = jax.nn.log_sigmoid(decay_logit.astype(F32))
    pos = jnp.arange(CHUNK, dtype=F32)
    rel = pos[:, None] - pos[None, :]
    out = []
    for d in range(2):
        l = lg[d]
        if d == 0:
            xi_e, zeta_e, r = pos + 1.0, CHUNK - 1.0 - pos, rel
        else:
            xi_e, zeta_e, r = CHUNK - pos, pos, -rel
        xi = jnp.repeat(jnp.exp(xi_e[:, None] * l), RET_QK_DIM, axis=1)
        zeta = jnp.repeat(jnp.exp(zeta_e[:, None] * l), RET_QK_DIM, axis=1)
        dmat = jnp.exp(jnp.where((r >= 0)[None], r[None] * l[:, None, None], -jnp.inf))
        cdec = jnp.broadcast_to(jnp.exp(CHUNK * l)[:, None, None], (RET_HEADS, 1, RET_V_DIM))
        out.append((xi, zeta, dmat, cdec))
    return out


def _rope_tables(n_rows, ctx_len):
    rows, cols = jnp.meshgrid(jnp.arange(n_rows), jnp.arange(GRID_W), indexing='ij')
    rows = rows.reshape(-1).astype(F32)
    cols = cols.reshape(-1).astype(F32)
    n_freq = RET_QK_DIM // 4
    inv_freq = ROPE_BASE ** (-jnp.arange(n_freq, dtype=F32) / n_freq)
    ang = jnp.concatenate([rows[:, None] * inv_freq, cols[:, None] * inv_freq], -1)
    cos = jnp.concatenate([jnp.ones((ctx_len, ang.shape[1]), F32), jnp.cos(ang)], 0)
    sin = jnp.concatenate([jnp.zeros((ctx_len, ang.shape[1]), F32), jnp.sin(ang)], 0)
    return cos, sin


def _layer_norm(t, g, b):
    mu = jnp.mean(t, axis=-1, keepdims=True)
    var = jnp.mean(jnp.square(t - mu), axis=-1, keepdims=True)
    return (t - mu) * lax.rsqrt(var + LN_EPS) * g + b


def _out_kernel(y_ref, x_ref, mod_ref, w_ref, lng_ref, lnb_ref, xn_ref, tok_ref):
    m = mod_ref[0]
    o = _dot(y_ref[0], w_ref[...])
    xn = _layer_norm(DEEPNORM_ALPHA * x_ref[0] + m[2:3, :] * o, lng_ref[...], lnb_ref[...])
    xn_ref[0] = xn
    tok_ref[0] = xn * (1.0 + m[4:5, :]) + m[3:4, :]


def _out_proj(y, xc, mods, w, ln_g, ln_b, bsz):
    _, lt, d = xc.shape
    row = lambda wd: pl.BlockSpec((1, ROW_BLOCK, wd), lambda b, i: (b, i, 0))
    vec = pl.BlockSpec((1, d), lambda b, i: (0, 0))
    return pl.pallas_call(
        _out_kernel,
        out_shape=(jax.ShapeDtypeStruct((bsz, lt, d), F32), jax.ShapeDtypeStruct((bsz, lt, d), F32)),
        grid=(bsz, lt // ROW_BLOCK),
        in_specs=[row(y.shape[-1]), row(d), _mod_spec(bsz), pl.BlockSpec(w.shape, lambda b, i: (0, 0)),
                  vec, vec],
        out_specs=(row(d), row(d)),
        compiler_params=_cparams(("parallel", "parallel")),
        name="out_proj_norm",
    )(y, xc, mods, w, ln_g, ln_b)


_META_E, _META_RANK, _META_W = 0, 2, 4


def _router_kernel(tok_ref, w_ref, b_ref, meta_ref, cnt_ref, carry_ref):
    @pl.when(pl.program_id(0) == 0)
    def _():
        carry_ref[...] = jnp.zeros(carry_ref.shape, F32)

    logits = _dot_hi(tok_ref[...], w_ref[...]) + b_ref[...]
    shape = logits.shape
    lane = lax.broadcasted_iota(jnp.int32, shape, 1).astype(F32)
    big = float(LANES)

    def first_max(vals):
        m = jnp.max(vals, axis=-1, keepdims=True)
        return m, jnp.min(jnp.where(vals == m, lane, big), axis=-1, keepdims=True)

    gl = jnp.where(lane < MOE_GROUPS, logits, -jnp.inf)
    gmax, gsel = first_max(gl)
    g_gate = 1.0 / jnp.sum(jnp.exp(gl - gmax), axis=-1, keepdims=True)
    lo = MOE_GROUPS + gsel * MOE_EXPERTS_PER_GROUP
    el = jnp.where((lane >= lo) & (lane < lo + MOE_EXPERTS_PER_GROUP), logits, -jnp.inf)
    m1, i1 = first_max(el)
    m2, i2 = first_max(jnp.where(lane == i1, -jnp.inf, el))
    e2 = jnp.exp(m2 - m1)
    w1 = g_gate / (1.0 + e2)
    w2 = g_gate * e2 / (1.0 + e2)
    e1, e2id = i1 - MOE_GROUPS, i2 - MOE_GROUPS

    oh1 = jnp.where(lane == e1, 1.0, 0.0)
    oh2 = jnp.where(lane == e2id, 1.0, 0.0)
    ohs = oh1 + oh2
    n = shape[0]
    r = lax.broadcasted_iota(jnp.int32, (n, n), 0)
    c = lax.broadcasted_iota(jnp.int32, (n, n), 1)
    before = _dot(jnp.where(c < r, 1.0, 0.0).astype(BF16), ohs.astype(BF16)) + carry_ref[...]
    rank1 = jnp.sum(oh1 * before, axis=-1, keepdims=True)
    rank2 = jnp.sum(oh2 * before, axis=-1, keepdims=True)
    carry_ref[...] = carry_ref[...] + jnp.sum(ohs, axis=0, keepdims=True)
    cnt_ref[...] = carry_ref[...]

    rec = jnp.zeros(shape, F32)
    for k, val in enumerate((e1, e2id, rank1, rank2, w1, w2)):
        rec = jnp.where(lane == float(k), val, rec)
    meta_ref[...] = rec


def _router(tokens, w_r, b_r):
    t, d = tokens.shape
    return pl.pallas_call(
        _router_kernel,
        out_shape=(jax.ShapeDtypeStruct((t, LANES), F32), jax.ShapeDtypeStruct((1, LANES), F32)),
        grid=(t // ROW_BLOCK,),
        in_specs=[pl.BlockSpec((ROW_BLOCK, d), lambda i: (i, 0)),
                  pl.BlockSpec((d, LANES), lambda i: (0, 0)),
                  pl.BlockSpec((1, LANES), lambda i: (0, 0))],
        out_specs=(pl.BlockSpec((ROW_BLOCK, LANES), lambda i: (i, 0)),
                   pl.BlockSpec((1, LANES), lambda i: (0, 0))),
        scratch_shapes=[pltpu.VMEM((1, LANES), F32)],
        compiler_params=_cparams(("arbitrary",)),
        name="moe_router",
    )(tokens, w_r, b_r)


def _row_copy(src_ref, src_row, dst_ref, dst_row, sem):
    return pltpu.make_async_copy(src_ref.at[pl.ds(src_row, 1), :], dst_ref.at[pl.ds(dst_row, 1), :], sem)


def _for_block_rows(fn):
    def body(rb, carry):
        for u in range(ROW_DMA_UNROLL):
            for k in range(MOE_TOP_K):
                fn(rb * ROW_DMA_UNROLL + u, k)
        return carry
    lax.fori_loop(0, ROW_BLOCK // ROW_DMA_UNROLL, body, 0)


def _dispatch_kernel(dest_ref, tok_ref, buf_in_ref, buf_ref, stage_ref, load_sem, row_sem):
    del buf_in_ref
    i = pl.program_id(0)
    n = pl.num_programs(0)

    def load(blk):
        slot = blk % 3
        return pltpu.make_async_copy(tok_ref.at[pl.ds(blk * ROW_BLOCK, ROW_BLOCK), :], stage_ref.at[slot],
                                     load_sem.at[slot])

    @pl.when(i == 0)
    def _():
        load(i).start()

    @pl.when(i + 1 < n)
    def _():
        load(i + 1).start()

    load(i).wait()
    stage = stage_ref.at[i % 3]
    base = i * ROW_BLOCK
    _for_block_rows(lambda r, k: _row_copy(stage, r, buf_ref, dest_ref[(base + r) * MOE_TOP_K + k],
                                           row_sem.at[i % 2]).start(priority=k))

    def drain(blk):
        _for_block_rows(lambda r, k: _row_copy(stage_ref.at[0], 0, buf_ref, 0, row_sem.at[blk % 2]).wait())

    @pl.when(i > 0)
    def _():
        drain(i - 1)

    @pl.when(i == n - 1)
    def _():
        drain(i)


def _dispatch(dest_flat, tokens, buf_prev):
    t, d = tokens.shape
    return pl.pallas_call(
        _dispatch_kernel,
        out_shape=jax.ShapeDtypeStruct(buf_prev.shape, F32),
        grid_spec=pltpu.PrefetchScalarGridSpec(
            num_scalar_prefetch=1,
            grid=(t // ROW_BLOCK,),
            in_specs=[pl.BlockSpec(memory_space=pl.ANY), pl.BlockSpec(memory_space=pl.ANY)],
            out_specs=pl.BlockSpec(memory_space=pl.ANY),
            scratch_shapes=[pltpu.VMEM((3, ROW_BLOCK, d), F32), pltpu.SemaphoreType.DMA((3,)),
                            pltpu.SemaphoreType.DMA((2,))],
        ),
        input_output_aliases={2: 0},
        compiler_params=_cparams(("arbitrary",)),
        name="moe_dispatch",
    )(dest_flat, tokens, buf_prev)


def _expert_kernel(be_ref, nused_ref, x_ref, wgu_ref, wd_ref, o_ref, wgu_bf_ref, wd_bf_ref):
    i = pl.program_id(0)

    @pl.when((i == 0) | (be_ref[i] != be_ref[jnp.maximum(i - 1, 0)]))
    def _():
        wgu_bf_ref[...] = wgu_ref[...].astype(BF16)
        wd_bf_ref[...] = wd_ref[...].astype(BF16)

    @pl.when(i < nused_ref[0])
    def _():
        gu = _dot(x_ref[...].astype(BF16), wgu_bf_ref[...])
        act = _silu(gu[:, :MOE_HIDDEN]) * gu[:, MOE_HIDDEN:]
        o_ref[...] = _dot(act.astype(BF16), wd_bf_ref[...])

    @pl.when(i >= nused_ref[0])
    def _():
        o_ref[...] = jnp.zeros(o_ref.shape, F32)


def _experts(block_e, n_used, buf, w_gu, w_d, layer):
    n_rows, d = buf.shape
    return pl.pallas_call(
        _expert_kernel,
        out_shape=jax.ShapeDtypeStruct((n_rows, d), F32),
        grid_spec=pltpu.PrefetchScalarGridSpec(
            num_scalar_prefetch=2,
            grid=(n_rows // EXPERT_BLOCK,),
            in_specs=[pl.BlockSpec((EXPERT_BLOCK, d), lambda i, be, nu: (i, 0)),
                      pl.BlockSpec((None, None, d, 2 * MOE_HIDDEN), lambda i, be, nu: (layer, be[i], 0, 0)),
                      pl.BlockSpec((None, None, MOE_HIDDEN, d), lambda i, be, nu: (layer, be[i], 0, 0))],
            out_specs=pl.BlockSpec((EXPERT_BLOCK, d), lambda i, be, nu: (i, 0)),
            scratch_shapes=[pltpu.VMEM((d, 2 * MOE_HIDDEN), BF16), pltpu.VMEM((MOE_HIDDEN, d), BF16)],
        ),
        compiler_params=_cparams(("arbitrary",)),
        name="moe_experts",
    )(block_e, n_used, buf, w_gu, w_d)


def _combine_kernel(dest_ref, eo_ref, meta_ref, x_ref, mod_ref, lng_ref, lnb_ref, out_ref, g_ref, sem):
    nblk = pl.num_programs(1)
    f = pl.program_id(0) * nblk + pl.program_id(1)
    n = pl.num_programs(0) * nblk

    def gather(blk):
        slot = blk % 2
        base = blk * ROW_BLOCK
        _for_block_rows(lambda r, k: _row_copy(eo_ref, dest_ref[(base + r) * MOE_TOP_K + k], g_ref.at[slot, k],
                                               r, sem.at[slot]).start(priority=k))

    @pl.when(f == 0)
    def _():
        gather(f)

    @pl.when(f + 1 < n)
    def _():
        gather(f + 1)

    slot = f % 2
    _for_block_rows(lambda r, k: _row_copy(eo_ref, 0, g_ref.at[slot, k], 0, sem.at[slot]).wait())

    meta = meta_ref[...]
    ffn = g_ref[slot, 0] * meta[:, _META_W:_META_W + 1] + g_ref[slot, 1] * meta[:, _META_W + 1:_META_W + 2]
    m = mod_ref[0]
    out_ref[0] = _layer_norm(DEEPNORM_ALPHA * x_ref[0] + m[5:6, :] * ffn, lng_ref[...], lnb_ref[...])


def _combine(dest_flat, eo, meta, xn, mods, ln_g, ln_b, bsz, drop_rows):
    _, lt, d = xn.shape
    nblk = lt // ROW_BLOCK
    drop = drop_rows // ROW_BLOCK
    return pl.pallas_call(
        _combine_kernel,
        out_shape=jax.ShapeDtypeStruct((bsz, lt - drop_rows, d), F32),
        grid_spec=pltpu.PrefetchScalarGridSpec(
            num_scalar_prefetch=1,
            grid=(bsz, nblk),
            in_specs=[pl.BlockSpec(memory_space=pl.ANY),
                      pl.BlockSpec((ROW_BLOCK, LANES), lambda b, i, dest: (b * nblk + i, 0)),
                      pl.BlockSpec((1, ROW_BLOCK, d), lambda b, i, dest: (b, i, 0)),
                      pl.BlockSpec((1, 6, d), lambda b, i, dest: (jnp.where(i == 0, bsz, b), 0, 0)),
                      pl.BlockSpec((1, d), lambda b, i, dest: (0, 0)),
                      pl.BlockSpec((1, d), lambda b, i, dest: (0, 0))],
            out_specs=pl.BlockSpec((1, ROW_BLOCK, d), lambda b, i, dest: (b, jnp.maximum(i - drop, 0), 0)),
            scratch_shapes=[pltpu.VMEM((2, MOE_TOP_K, ROW_BLOCK, d), F32), pltpu.SemaphoreType.DMA((2,))],
        ),
        compiler_params=_cparams(("arbitrary", "arbitrary")),
        name="moe_combine",
    )(dest_flat, eo, meta, xn, mods, ln_g, ln_b)


def _moe_rows(n_tokens):
    return (MOE_TOP_K * n_tokens + MOE_EXPERTS * (EXPERT_BLOCK - 1)) // EXPERT_BLOCK * EXPERT_BLOCK


def _moe(tokens3, xn, mods, w_r, b_r, w_gu, w_d, layer, ln_g, ln_b, bsz, buf_prev, drop_rows):
    _, lt, d = tokens3.shape
    tokens = tokens3.reshape(bsz * lt, d)
    n_blocks = buf_prev.shape[0] // EXPERT_BLOCK
    meta, cnt = _router(tokens, w_r, b_r)
    eid = meta[:, _META_E:_META_E + MOE_TOP_K].astype(jnp.int32)
    rank = meta[:, _META_RANK:_META_RANK + MOE_TOP_K].astype(jnp.int32)
    counts = cnt[0, :MOE_EXPERTS].astype(jnp.int32)
    padded = (counts + EXPERT_BLOCK - 1) // EXPERT_BLOCK * EXPERT_BLOCK
    pends = jnp.cumsum(padded)
    experts = jnp.arange(MOE_EXPERTS, dtype=jnp.int32)
    run_start = jnp.sum(jnp.where(eid[..., None] == experts, pends - padded, 0), axis=-1)
    dest_flat = (run_start + rank).reshape(-1)
    block_start = jnp.arange(n_blocks, dtype=jnp.int32) * EXPERT_BLOCK
    block_e = jnp.minimum(jnp.sum((pends[None, :] <= block_start[:, None]).astype(jnp.int32), axis=1),
                          MOE_EXPERTS - 1)
    n_used = (pends[-1:] // EXPERT_BLOCK).astype(jnp.int32)
    buf = _dispatch(dest_flat, tokens, buf_prev)
    eo = _experts(block_e, n_used, buf, w_gu, w_d, layer)
    return _combine(dest_flat, eo, meta, xn, mods, ln_g, ln_b, bsz, drop_rows), buf


def kernel(x, c, ctx, c_ctx, mod_w, mod_b, ssd_in_w, ssd_conv_w, ssd_conv_b, ssd_dt_bias, ssd_a_log, ssd_d_skip, ssd_norm_w, ssd_out_w, ret_in_w, ret_decay_logit, ret_gn_w, ret_gn_b, ret_out_w, ln_mix_g, ln_mix_b, ln_ffn_g, ln_ffn_b, moe_group_w, moe_group_b, moe_expert_w, moe_expert_b, moe_w_gate_up, moe_w_down):
    bsz, seqlen, d = x.shape
    ctx_len = ctx.shape[1]
    assert d == D_MODEL and ctx_len == ROW_BLOCK and seqlen % ROW_BLOCK == 0 and seqlen % GRID_W == 0
    assert bsz + 1 <= 16
    depth = mod_w.shape[0]

    xc = jnp.concatenate([ctx, x], axis=1)
    cc = jnp.zeros((16, d), F32).at[:bsz].set(c).at[bsz].set(c_ctx)
    cos, sin = _rope_tables(seqlen // GRID_W, ctx_len)
    w_gu, w_d = moe_w_gate_up, moe_w_down
    buf = jnp.zeros((_moe_rows(bsz * (ctx_len + seqlen)), d), F32)

    for i in range(depth):
        j = i // 2
        mods = _modulation(cc, mod_w, mod_b, i)
        if i % 2 == 0:
            w = ssd_in_w[j]
            n_main = SSD_D_INNER + SSD_CONV_DIM
            w_main = w[:, :n_main].astype(BF16)
            w_dt = jnp.concatenate([w[:, n_main:], w[:, n_main:]], axis=1).astype(BF16)
            dtb = jnp.tile(ssd_dt_bias[j].reshape(1, -1), (1, 2))
            alog = jnp.tile(ssd_a_log[j].reshape(1, -1), (1, 2))
            z, xbc, dtl = _ssd_proj(xc, mods, w_main, w_dt, dtb, alog, ssd_conv_w[j],
                                    ssd_conv_b[j].reshape(1, -1), bsz)
            y_f = _ssd_scan(xbc, dtl, 0, ctx_len)
            dskip = jnp.repeat(ssd_d_skip[j], SSD_HEADDIM).reshape(1, -1)
            y = _ssd_scan(xbc, dtl, 1, ctx_len, z=z, y_f=y_f, dskip=dskip,
                          norm_w=ssd_norm_w[j].reshape(1, -1))
            out_w = ssd_out_w[j]
        else:
            q, k, v, g = _ret_proj(xc, mods, ret_in_w[j].astype(BF16), cos, sin, bsz)
            tabs = _ret_tables(ret_decay_logit[j])
            o_f = _ret_scan(q, k, v, tabs[0], 0, ctx_len)
            y = _ret_scan(q, k, v, tabs[1], 1, ctx_len, g=g, o_f=o_f,
                          gn_w=ret_gn_w[j].reshape(1, -1), gn_b=ret_gn_b[j].reshape(1, -1))
            out_w = ret_out_w[j]
        xn, tokens = _out_proj(y, xc, mods, out_w.astype(BF16), ln_mix_g[i].reshape(1, -1),
                               ln_mix_b[i].reshape(1, -1), bsz)
        w_r = jnp.zeros((d, LANES), F32).at[:, :MOE_GROUPS].set(moe_group_w[i])
        w_r = w_r.at[:, MOE_GROUPS:MOE_GROUPS + MOE_EXPERTS].set(moe_expert_w[i])
        b_r = jnp.zeros((1, LANES), F32).at[0, :MOE_GROUPS].set(moe_group_b[i])
        b_r = b_r.at[0, MOE_GROUPS:MOE_GROUPS + MOE_EXPERTS].set(moe_expert_b[i])
        xc, buf = _moe(tokens, xn, mods, w_r, b_r, w_gu, w_d, i, ln_ffn_g[i].reshape(1, -1),
                       ln_ffn_b[i].reshape(1, -1), bsz, buf, ctx_len if i == depth - 1 else 0)
    return xc
```

```python
import functools

import jax
import jax.numpy as jnp
from jax import lax
from jax.experimental import pallas as pl
from jax.experimental.pallas import tpu as pltpu

F32 = jnp.float32
BF16 = jnp.bfloat16

D_MODEL = 1024
DEPTH = 4
GRID_W = 64
DEEPNORM_ALPHA = (2.0 * DEPTH) ** 0.25
LN_EPS = 1e-5

SSD_D_INNER = 2 * D_MODEL
SSD_HEADDIM = 64
SSD_HEADS = SSD_D_INNER // SSD_HEADDIM
SSD_GROUPS = 4
SSD_HPG = SSD_HEADS // SSD_GROUPS
SSD_STATE = 128
SSD_CONV_W = 5
SSD_BC_DIM = SSD_GROUPS * SSD_STATE
SSD_CONV_DIM = SSD_D_INNER + 2 * SSD_BC_DIM
SSD_GROUP_W = SSD_HPG * SSD_HEADDIM

RET_HEADS = D_MODEL // 256
RET_QK_DIM = D_MODEL // RET_HEADS
RET_VALUE = 2 * D_MODEL
RET_V_DIM = RET_VALUE // RET_HEADS
ROPE_BASE = 10000.0

MOE_GROUPS = 4
MOE_EXPERTS_PER_GROUP = 8
MOE_EXPERTS = MOE_GROUPS * MOE_EXPERTS_PER_GROUP
MOE_HIDDEN = D_MODEL // 2
MOE_TOP_K = 2

CHUNK = 128
LANES = 128
ROW_BLOCK = 256
EXPERT_BLOCK = 512
SCAN_SUB = ROW_BLOCK // CHUNK
CONV_HALO = 8
CONV_COLS = 256
ROW_DMA_UNROLL = 8
COMBINE_TILE = 32
ROUTER_BLOCK = 512
VMEM_LIMIT = 48 * 1024 * 1024
NEG_BIG = -1e30


def _cparams(sem):
    return pltpu.CompilerParams(dimension_semantics=sem, vmem_limit_bytes=VMEM_LIMIT)


def _dot(a, b):
    return jnp.dot(a, b, preferred_element_type=F32)


def _split2(x):
    hi = x.astype(BF16)
    lo = (x - hi.astype(F32)).astype(BF16)
    return hi, lo


def _split3(x):
    hi = x.astype(BF16)
    r = x - hi.astype(F32)
    mid = r.astype(BF16)
    lo = (r - mid.astype(F32)).astype(BF16)
    return hi, mid, lo


def _dot_hi(a, b):
    ah, al = _split2(a)
    bh, bl = _split2(b)
    return _dot(ah, bh) + (_dot(ah, bl) + _dot(al, bh))


def _silu(x):
    return x * jax.nn.sigmoid(x)


def _softplus(x):
    return jnp.maximum(x, 0.0) + jnp.log1p(jnp.exp(-jnp.abs(x)))


def _mod_kernel(c_ref, w_ref, b_ref, o_ref):
    o_ref[...] = _dot_hi(_silu(c_ref[...]), w_ref[...]) + b_ref[...]


def _modulation(cc, mod_w, mod_b, layer):
    rows, d = cc.shape
    n = mod_w.shape[-1]
    tn = 1024
    out = pl.pallas_call(
        _mod_kernel,
        out_shape=jax.ShapeDtypeStruct((rows, n), F32),
        grid=(n // tn,),
        in_specs=[pl.BlockSpec((rows, d), lambda j: (0, 0)),
                  pl.BlockSpec((None, d, tn), lambda j: (layer, 0, j)),
                  pl.BlockSpec((None, 1, tn), lambda j: (layer, 0, j))],
        out_specs=pl.BlockSpec((rows, tn), lambda j: (0, j)),
        compiler_params=_cparams(("parallel",)),
        name="modulation",
    )(cc, mod_w, mod_b.reshape(mod_b.shape[0], 1, n))
    return out.reshape(rows, 6, d)


def _mod_spec(bsz):
    return pl.BlockSpec((1, 6, D_MODEL), lambda b, i: (jnp.where(i == 0, bsz, b), 0, 0))


def _ssd_proj_kernel(x_ref, xlo_ref, xhi_ref, mod_ref, w_ref, wdt_ref, dtb_ref, alog_ref, cw_ref, cb_ref,
                     z_ref, xbc_ref, dtl_ref):
    i = pl.program_id(1)
    m = mod_ref[0]
    main = slice(CONV_HALO, CONV_HALO + ROW_BLOCK)
    n_all = ROW_BLOCK + 2 * CONV_HALO
    x_all = jnp.concatenate([xlo_ref[0], x_ref[0], xhi_ref[0]], axis=0)
    u = (x_all * (1.0 + m[1:2, :]) + m[0:1, :]).astype(BF16)
    lo_ok = jnp.where(i >= 2, 1.0, 0.0)
    hi_ok = jnp.where((i >= 1) & (i < pl.num_programs(1) - 1), 1.0, 0.0)
    half = SSD_CONV_W // 2
    for c0 in range(0, SSD_CONV_DIM, CONV_COLS):
        cs = slice(c0, c0 + CONV_COLS)
        v = _dot(u, w_ref[:, SSD_D_INNER + c0:SSD_D_INNER + c0 + CONV_COLS])
        v = jnp.concatenate([v[0:CONV_HALO] * lo_ok, v[main], v[CONV_HALO + ROW_BLOCK:n_all] * hi_ok], axis=0)
        if c0 < SSD_D_INNER:
            z_ref[0, :, cs] = _dot(u, w_ref[:, cs])[main]
        acc = cb_ref[:, cs] + cw_ref[half:half + 1, cs] * v[main]
        for k in range(SSD_CONV_W):
            if k != half:
                acc = acc + cw_ref[k:k + 1, cs] * pltpu.roll(v, (half - k) % n_all, 0)[main]
        xbc_ref[0, :, cs] = _silu(acc)
    dt = _softplus(_dot(u, wdt_ref[...])[main] + dtb_ref[...])
    lane = lax.broadcasted_iota(jnp.int32, dt.shape, 1)
    dtl_ref[0] = jnp.where(lane < 2 * SSD_HEADS, dt, dt * (-jnp.exp(alog_ref[...])))


def _ssd_proj(xc, mods, w_main, w_dt, dtb, alog, conv_w, conv_b, bsz):
    _, lt, d = xc.shape
    nblk = lt // ROW_BLOCK
    per = ROW_BLOCK // CONV_HALO
    row = lambda w: pl.BlockSpec((1, ROW_BLOCK, w), lambda b, i: (b, i, 0))
    full = lambda a: pl.BlockSpec(a.shape, lambda b, i: (0,) * a.ndim)
    lo = pl.BlockSpec((1, CONV_HALO, d), lambda b, i: (b, jnp.maximum(i * per - 1, 0), 0))
    hi = pl.BlockSpec((1, CONV_HALO, d), lambda b, i: (b, jnp.minimum((i + 1) * per, nblk * per - 1), 0))
    return pl.pallas_call(
        _ssd_proj_kernel,
        out_shape=(jax.ShapeDtypeStruct((bsz, lt, SSD_D_INNER), F32),
                   jax.ShapeDtypeStruct((bsz, lt, SSD_CONV_DIM), F32),
                   jax.ShapeDtypeStruct((bsz, lt, LANES), F32)),
        grid=(bsz, nblk),
        in_specs=[row(d), lo, hi, _mod_spec(bsz), full(w_main), full(w_dt), full(dtb), full(alog),
                  full(conv_w), full(conv_b)],
        out_specs=(row(SSD_D_INNER), row(SSD_CONV_DIM), row(LANES)),
        compiler_params=_cparams(("parallel", "parallel")),
        name="ssd_in_proj",
    )(xc, xc, xc, mods, w_main, w_dt, dtb, alog, conv_w, conv_b)


def _scan_block_index(direction, ctx_blocks, n_blocks):
    if direction == 0:
        return lambda j: j
    return lambda j: jnp.where(j < ctx_blocks, ctx_blocks - 1 - j, n_blocks - 1 + ctx_blocks - j)


def _scan_rows(direction, s):
    c = s if direction == 0 else SCAN_SUB - 1 - s
    return slice(c * CHUNK, (c + 1) * CHUNK)


def _ssd_scan_kernel(*refs, direction, final):
    if final:
        xs_ref, b_ref, c_ref, dtl_ref, z_ref, yf_ref, dskip_ref, nw_ref, out_ref, h_ref, y_ref = refs
    else:
        xs_ref, b_ref, c_ref, dtl_ref, out_ref, h_ref = refs

    @pl.when(pl.program_id(1) == 0)
    def _():
        h_ref[...] = jnp.zeros(h_ref.shape, F32)

    row = lax.broadcasted_iota(jnp.int32, (CHUNK, CHUNK), 0)
    col = lax.broadcasted_iota(jnp.int32, (CHUNK, CHUNK), 1)
    if direction == 0:
        mask, end = row >= col, CHUNK - 1
    else:
        mask, end = col >= row, 0
    tri = jnp.where(mask, 1.0, 0.0).astype(BF16)
    tri_t = jnp.where(mask, 0.0, 1.0)
    tri_t = jnp.where(row == col, 1.0, tri_t).astype(BF16)
    lane_lo = col < SSD_HEADDIM
    o_dt = SSD_HEADS * direction
    o_la = 2 * SSD_HEADS + SSD_HEADS * direction

    for s in range(SCAN_SUB):
        rows = _scan_rows(direction, s)
        dtl = dtl_ref[0, rows, :]
        dtl_t = dtl.T
        p0, p1, p2 = _split3(dtl)
        cum = _dot(tri, p0) + _dot(tri, p1) + _dot(tri, p2)
        q0, q1, q2 = _split3(dtl_t)
        cum_t = _dot(q0, tri_t) + _dot(q1, tri_t) + _dot(q2, tri_t)
        dt_t = dtl_t[o_dt:o_dt + SSD_HEADS, :]
        a_t = cum_t[o_la:o_la + SSD_HEADS, :]
        w1_t = dt_t * jnp.exp(a_t[:, end:end + 1] - a_t)
        a_dt_t = a_t - jnp.log(dt_t)

        for g in range(SSD_GROUPS):
            gs = slice(g * SSD_STATE, (g + 1) * SSD_STATE)
            bg = b_ref[0, rows, gs]
            cg = c_ref[0, rows, gs].astype(BF16)
            cb = lax.dot_general(cg, bg.astype(BF16), (((1,), (1,)), ((), ())), preferred_element_type=F32)
            bg_t = bg.T
            h_in = h_ref[g]
            y_off = _dot(cg, h_in.astype(BF16))
            for jp in range(SSD_HPG // 2):
                ls = slice(g * SSD_GROUP_W + jp * LANES, g * SSD_GROUP_W + (jp + 1) * LANES)
                gl = slice(jp * LANES, (jp + 1) * LANES)
                xp = xs_ref[0, rows, ls]
                top, bot, acols = [], [], []
                for e in range(2):
                    h = g * SSD_HPG + 2 * jp + e
                    a_col = jnp.broadcast_to(cum[:, o_la + h:o_la + h + 1], (CHUNK, CHUNK))
                    seg = jnp.where(mask, a_col - a_dt_t[h:h + 1, :], NEG_BIG)
                    top.append(cb * jnp.exp(seg))
                    bot.append(bg_t * w1_t[h:h + 1, :])
                    acols.append(a_col)
                lhs = jnp.concatenate([jnp.concatenate(top, axis=1), jnp.concatenate(bot, axis=1)], axis=0)
                rhs = jnp.concatenate([jnp.where(lane_lo, xp, 0.0), jnp.where(lane_lo, 0.0, xp)], axis=0)
                res = _dot(lhs.astype(BF16), rhs.astype(BF16))
                ea = jnp.exp(jnp.where(lane_lo, acols[0], acols[1]))
                y_pair = res[0:CHUNK] + y_off[:, gl] * ea
                if final:
                    y_ref[:, ls] = y_pair
                else:
                    out_ref[0, rows, ls] = y_pair
                h_ref[g, :, gl] = h_in[:, gl] * ea[end:end + 1, :] + res[CHUNK:2 * CHUNK]

        if final:
            y = y_ref[...] + yf_ref[0, rows, :] + xs_ref[0, rows, :] * dskip_ref[...]
            y = y * _silu(z_ref[0, rows, :])
            gw = SSD_D_INNER // SSD_GROUPS
            for g in range(SSD_GROUPS):
                gs = slice(g * gw, (g + 1) * gw)
                yg = y[:, gs]
                ms = jnp.mean(yg * yg, axis=-1, keepdims=True)
                out_ref[0, rows, gs] = (yg * lax.rsqrt(ms + LN_EPS) * nw_ref[:, gs]).astype(out_ref.dtype)


def _ssd_scan(xbc, dtl, direction, ctx_len, z=None, y_f=None, dskip=None, norm_w=None):
    bsz, lt, _ = xbc.shape
    bi = _scan_block_index(direction, ctx_len // ROW_BLOCK, lt // ROW_BLOCK)
    final = z is not None
    col = lambda w, cblk: pl.BlockSpec((1, ROW_BLOCK, w), lambda b, j: (b, bi(j), cblk))
    in_specs = [col(SSD_D_INNER, 0), col(SSD_BC_DIM, SSD_D_INNER // SSD_BC_DIM),
                col(SSD_BC_DIM, SSD_D_INNER // SSD_BC_DIM + 1), col(LANES, 0)]
    args = [xbc, xbc, xbc, dtl]
    scratch = [pltpu.VMEM((SSD_GROUPS, SSD_STATE, SSD_GROUP_W), F32)]
    if final:
        vec = pl.BlockSpec((1, SSD_D_INNER), lambda b, j: (0, 0))
        in_specs += [col(SSD_D_INNER, 0), col(SSD_D_INNER, 0), vec, vec]
        args += [z, y_f, dskip, norm_w]
        scratch.append(pltpu.VMEM((CHUNK, SSD_D_INNER), F32))
    return pl.pallas_call(
        functools.partial(_ssd_scan_kernel, direction=direction, final=final),
        out_shape=jax.ShapeDtypeStruct((bsz, lt, SSD_D_INNER), BF16 if final else F32),
        grid=(bsz, lt // ROW_BLOCK),
        in_specs=in_specs,
        out_specs=col(SSD_D_INNER, 0),
        scratch_shapes=scratch,
        compiler_params=_cparams(("parallel", "arbitrary")),
        name="ssd_scan_bwd" if final else "ssd_scan_fwd",
    )(*args)


def _ret_proj_kernel(x_ref, mod_ref, w_ref, cos_ref, sin_ref, q_ref, k_ref, v_ref, g_ref):
    m = mod_ref[0]
    u = (x_ref[0] * (1.0 + m[1:2, :]) + m[0:1, :]).astype(BF16)
    cs, sn = cos_ref[...], sin_ref[...]
    half = RET_QK_DIM // 2

    def rope(t, out_ref):
        for h in range(RET_HEADS):
            t1 = t[:, h * RET_QK_DIM:h * RET_QK_DIM + half]
            t2 = t[:, h * RET_QK_DIM + half:(h + 1) * RET_QK_DIM]
            out_ref[0, :, h * RET_QK_DIM:h * RET_QK_DIM + half] = t1 * cs - t2 * sn
            out_ref[0, :, h * RET_QK_DIM + half:(h + 1) * RET_QK_DIM] = t1 * sn + t2 * cs

    rope(_dot(u, w_ref[:, 0:D_MODEL]), q_ref)
    rope(_dot(u, w_ref[:, D_MODEL:2 * D_MODEL]) * (RET_QK_DIM ** -0.5), k_ref)
    v_ref[0] = _dot(u, w_ref[:, 2 * D_MODEL:2 * D_MODEL + RET_VALUE])
    g_ref[0] = _dot(u, w_ref[:, 2 * D_MODEL + RET_VALUE:2 * D_MODEL + 2 * RET_VALUE])


def _ret_proj(xc, mods, w, cos, sin, bsz):
    _, lt, d = xc.shape
    row = lambda wd: pl.BlockSpec((1, ROW_BLOCK, wd), lambda b, i: (b, i, 0))
    tab = pl.BlockSpec((ROW_BLOCK, RET_QK_DIM // 2), lambda b, i: (i, 0))
    return pl.pallas_call(
        _ret_proj_kernel,
        out_shape=(jax.ShapeDtypeStruct((bsz, lt, D_MODEL), F32),
                   jax.ShapeDtypeStruct((bsz, lt, D_MODEL), F32),
                   jax.ShapeDtypeStruct((bsz, lt, RET_VALUE), F32),
                   jax.ShapeDtypeStruct((bsz, lt, RET_VALUE), F32)),
        grid=(bsz, lt // ROW_BLOCK),
        in_specs=[row(d), _mod_spec(bsz), pl.BlockSpec(w.shape, lambda b, i: (0, 0)), tab, tab],
        out_specs=(row(D_MODEL), row(D_MODEL), row(RET_VALUE), row(RET_VALUE)),
        compiler_params=_cparams(("parallel", "parallel")),
        name="ret_in_proj",
    )(xc, mods, w, cos, sin)


def _ret_scan_kernel(*refs, direction, final):
    if final:
        (q_ref, k_ref, v_ref, xi_ref, zeta_ref, dmat_ref, cdec_ref,
         g_ref, of_ref, gnw_ref, gnb_ref, out_ref, s_ref) = refs
    else:
        q_ref, k_ref, v_ref, xi_ref, zeta_ref, dmat_ref, cdec_ref, out_ref, s_ref = refs

    @pl.when(pl.program_id(1) == 0)
    def _():
        s_ref[...] = jnp.zeros(s_ref.shape, F32)

    for s in range(SCAN_SUB):
        rows = _scan_rows(direction, s)
        for h in range(RET_HEADS):
            qs = slice(h * RET_QK_DIM, (h + 1) * RET_QK_DIM)
            vs = slice(h * RET_V_DIM, (h + 1) * RET_V_DIM)
            qh, kh = q_ref[0, rows, qs], k_ref[0, rows, qs]
            vh = v_ref[0, rows, vs].astype(BF16)
            s_in = s_ref[h]
            sc = lax.dot_general(qh.astype(BF16), kh.astype(BF16), (((1,), (1,)), ((), ())),
                                 preferred_element_type=F32) * dmat_ref[h]
            o = _dot(sc.astype(BF16), vh) + _dot((qh * xi_ref[:, qs]).astype(BF16), s_in.astype(BF16))
            kz_t = (kh * zeta_ref[:, qs]).T
            s_ref[h] = s_in * cdec_ref[h] + _dot(kz_t.astype(BF16), vh)
            if final:
                o = o + of_ref[0, rows, vs]
                mu = jnp.mean(o, axis=-1, keepdims=True)
                var = jnp.mean(jnp.square(o - mu), axis=-1, keepdims=True)
                o = (o - mu) * lax.rsqrt(var + LN_EPS) * gnw_ref[:, vs] + gnb_ref[:, vs]
                out_ref[0, rows, vs] = (o * _silu(g_ref[0, rows, vs])).astype(out_ref.dtype)
            else:
                out_ref[0, rows, vs] = o


def _ret_scan(q, k, v, tabs, direction, ctx_len, g=None, o_f=None, gn_w=None, gn_b=None):
    bsz, lt, _ = q.shape
    bi = _scan_block_index(direction, ctx_len // ROW_BLOCK, lt // ROW_BLOCK)
    final = g is not None
    xi, zeta, dmat, cdec = tabs
    col = lambda w: pl.BlockSpec((1, ROW_BLOCK, w), lambda b, j: (b, bi(j), 0))
    full = lambda a: pl.BlockSpec(a.shape, lambda b, j: (0,) * a.ndim)
    in_specs = [col(D_MODEL), col(D_MODEL), col(RET_VALUE), full(xi), full(zeta), full(dmat), full(cdec)]
    args = [q, k, v, xi, zeta, dmat, cdec]
    if final:
        vec = pl.BlockSpec((1, RET_VALUE), lambda b, j: (0, 0))
        in_specs += [col(RET_VALUE), col(RET_VALUE), vec, vec]
        args += [g, o_f, gn_w, gn_b]
    return pl.pallas_call(
        functools.partial(_ret_scan_kernel, direction=direction, final=final),
        out_shape=jax.ShapeDtypeStruct((bsz, lt, RET_VALUE), BF16 if final else F32),
        grid=(bsz, lt // ROW_BLOCK),
        in_specs=in_specs,
        out_specs=col(RET_VALUE),
        scratch_shapes=[pltpu.VMEM((RET_HEADS, RET_QK_DIM, RET_V_DIM), F32)],
        compiler_params=_cparams(("parallel", "arbitrary")),
        name="ret_scan_bwd" if final else "ret_scan_fwd",
    )(*args)


def _ret_tables(decay_logit):
    lg = jax.nn.log_sigmoid(decay_logit.astype(F32))
    pos = jnp.arange(CHUNK, dtype=F32)
    rel = pos[:, None] - pos[None, :]
    out = []
    for d in range(2):
        l = lg[d]
        if d == 0:
            xi_e, zeta_e, r = pos + 1.0, CHUNK - 1.0 - pos, rel
        else:
            xi_e, zeta_e, r = CHUNK - pos, pos, -rel
        xi = jnp.repeat(jnp.exp(xi_e[:, None] * l), RET_QK_DIM, axis=1)
        zeta = jnp.repeat(jnp.exp(zeta_e[:, None] * l), RET_QK_DIM, axis=1)
        dmat = jnp.exp(jnp.where((r >= 0)[None], r[None] * l[:, None, None], -jnp.inf))
        cdec = jnp.broadcast_to(jnp.exp(CHUNK * l)[:, None, None], (RET_HEADS, 1, RET_V_DIM))
        out.append((xi, zeta, dmat, cdec))
    return out


def _rope_tables(n_rows, ctx_len):
    rows, cols = jnp.meshgrid(jnp.arange(n_rows), jnp.arange(GRID_W), indexing='ij')
    rows = rows.reshape(-1).astype(F32)
    cols = cols.reshape(-1).astype(F32)
    n_freq = RET_QK_DIM // 4
    inv_freq = ROPE_BASE ** (-jnp.arange(n_freq, dtype=F32) / n_freq)
    ang = jnp.concatenate([rows[:, None] * inv_freq, cols[:, None] * inv_freq], -1)
    cos = jnp.concatenate([jnp.ones((ctx_len, ang.shape[1]), F32), jnp.cos(ang)], 0)
    sin = jnp.concatenate([jnp.zeros((ctx_len, ang.shape[1]), F32), jnp.sin(ang)], 0)
    return cos, sin


def _layer_norm(t, g, b):
    mu = jnp.mean(t, axis=-1, keepdims=True)
    var = jnp.mean(jnp.square(t - mu), axis=-1, keepdims=True)
    return (t - mu) * lax.rsqrt(var + LN_EPS) * g + b


def _out_kernel(y_ref, x_ref, mod_ref, w_ref, lng_ref, lnb_ref, xn_ref, tok_ref):
    m = mod_ref[0]
    o = _dot(y_ref[0], w_ref[...])
    xn = _layer_norm(DEEPNORM_ALPHA * x_ref[0] + m[2:3, :] * o, lng_ref[...], lnb_ref[...])
    xn_ref[0] = xn
    tok_ref[0] = xn * (1.0 + m[4:5, :]) + m[3:4, :]


def _out_proj(y, xc, mods, w, ln_g, ln_b, bsz):
    _, lt, d = xc.shape
    row = lambda wd: pl.BlockSpec((1, ROW_BLOCK, wd), lambda b, i: (b, i, 0))
    vec = pl.BlockSpec((1, d), lambda b, i: (0, 0))
    return pl.pallas_call(
        _out_kernel,
        out_shape=(jax.ShapeDtypeStruct((bsz, lt, d), F32), jax.ShapeDtypeStruct((bsz, lt, d), F32)),
        grid=(bsz, lt // ROW_BLOCK),
        in_specs=[row(y.shape[-1]), row(d), _mod_spec(bsz), pl.BlockSpec(w.shape, lambda b, i: (0, 0)),
                  vec, vec],
        out_specs=(row(d), row(d)),
        compiler_params=_cparams(("parallel", "parallel")),
        name="out_proj_norm",
    )(y, xc, mods, w, ln_g, ln_b)


_META_E, _META_RANK, _META_W = 0, 2, 4


def _router_kernel(tok_ref, w_ref, b_ref, meta_ref, cnt_ref, carry_ref):
    @pl.when(pl.program_id(0) == 0)
    def _():
        carry_ref[...] = jnp.zeros(carry_ref.shape, F32)

    logits = _dot_hi(tok_ref[...], w_ref[...]) + b_ref[...]
    shape = logits.shape
    lane = lax.broadcasted_iota(jnp.int32, shape, 1).astype(F32)
    big = float(LANES)

    def first_max(vals):
        m = jnp.max(vals, axis=-1, keepdims=True)
        return m, jnp.min(jnp.where(vals == m, lane, big), axis=-1, keepdims=True)

    gl = jnp.where(lane < MOE_GROUPS, logits, -jnp.inf)
    gmax, gsel = first_max(gl)
    g_gate = 1.0 / jnp.sum(jnp.exp(gl - gmax), axis=-1, keepdims=True)
    lo = MOE_GROUPS + gsel * MOE_EXPERTS_PER_GROUP
    el = jnp.where((lane >= lo) & (lane < lo + MOE_EXPERTS_PER_GROUP), logits, -jnp.inf)
    m1, i1 = first_max(el)
    m2, i2 = first_max(jnp.where(lane == i1, -jnp.inf, el))
    e2 = jnp.exp(m2 - m1)
    w1 = g_gate / (1.0 + e2)
    w2 = g_gate * e2 / (1.0 + e2)
    e1, e2id = i1 - MOE_GROUPS, i2 - MOE_GROUPS

    oh1 = jnp.where(lane == e1, 1.0, 0.0)
    oh2 = jnp.where(lane == e2id, 1.0, 0.0)
    ohs = oh1 + oh2
    n = shape[0]
    r = lax.broadcasted_iota(jnp.int32, (n, n), 0)
    c = lax.broadcasted_iota(jnp.int32, (n, n), 1)
    before = _dot(jnp.where(c < r, 1.0, 0.0).astype(BF16), ohs.astype(BF16)) + carry_ref[...]
    rank1 = jnp.sum(oh1 * before, axis=-1, keepdims=True)
    rank2 = jnp.sum(oh2 * before, axis=-1, keepdims=True)
    carry_ref[...] = carry_ref[...] + jnp.sum(ohs, axis=0, keepdims=True)
    cnt_ref[...] = carry_ref[...]

    rec = jnp.zeros(shape, F32)
    for k, val in enumerate((e1, e2id, rank1, rank2, w1, w2)):
        rec = jnp.where(lane == float(k), val, rec)
    meta_ref[...] = rec


def _router(tokens, w_r, b_r):
    t, d = tokens.shape
    return pl.pallas_call(
        _router_kernel,
        out_shape=(jax.ShapeDtypeStruct((t, LANES), F32), jax.ShapeDtypeStruct((1, LANES), F32)),
        grid=(t // ROUTER_BLOCK,),
        in_specs=[pl.BlockSpec((ROUTER_BLOCK, d), lambda i: (i, 0)),
                  pl.BlockSpec((d, LANES), lambda i: (0, 0)),
                  pl.BlockSpec((1, LANES), lambda i: (0, 0))],
        out_specs=(pl.BlockSpec((ROUTER_BLOCK, LANES), lambda i: (i, 0)),
                   pl.BlockSpec((1, LANES), lambda i: (0, 0))),
        scratch_shapes=[pltpu.VMEM((1, LANES), F32)],
        compiler_params=_cparams(("arbitrary",)),
        name="moe_router",
    )(tokens, w_r, b_r)


def _row_copy(src_ref, src_row, dst_ref, dst_row, sem):
    return pltpu.make_async_copy(src_ref.at[pl.ds(src_row, 1), :], dst_ref.at[pl.ds(dst_row, 1), :], sem)


def _for_block_rows(fn):
    def body(rb, carry):
        for u in range(ROW_DMA_UNROLL):
            for k in range(MOE_TOP_K):
                fn(rb * ROW_DMA_UNROLL + u, k)
        return carry
    lax.fori_loop(0, ROW_BLOCK // ROW_DMA_UNROLL, body, 0)


def _dispatch_kernel(dest_ref, tok_ref, buf_in_ref, buf_ref, stage_ref, load_sem, row_sem):
    del buf_in_ref
    i = pl.program_id(0)
    n = pl.num_programs(0)

    def load(blk):
        slot = blk % 3
        return pltpu.make_async_copy(tok_ref.at[pl.ds(blk * ROW_BLOCK, ROW_BLOCK), :], stage_ref.at[slot],
                                     load_sem.at[slot])

    @pl.when(i == 0)
    def _():
        load(i).start()

    @pl.when(i + 1 < n)
    def _():
        load(i + 1).start()

    load(i).wait()
    stage = stage_ref.at[i % 3]
    base = i * ROW_BLOCK
    for r in range(ROW_BLOCK):
        for k in range(MOE_TOP_K):
            _row_copy(stage, r, buf_ref, dest_ref[(base + r) * MOE_TOP_K + k],
                      row_sem.at[i % 2]).start(priority=k)

    def drain(blk):
        _for_block_rows(lambda r, k: _row_copy(stage_ref.at[0], 0, buf_ref, 0, row_sem.at[blk % 2]).wait())

    @pl.when(i > 0)
    def _():
        drain(i - 1)

    @pl.when(i == n - 1)
    def _():
        drain(i)


def _dispatch(dest_flat, tokens, buf_prev):
    t, d = tokens.shape
    return pl.pallas_call(
        _dispatch_kernel,
        out_shape=jax.ShapeDtypeStruct(buf_prev.shape, F32),
        grid_spec=pltpu.PrefetchScalarGridSpec(
            num_scalar_prefetch=1,
            grid=(t // ROW_BLOCK,),
            in_specs=[pl.BlockSpec(memory_space=pl.ANY), pl.BlockSpec(memory_space=pl.ANY)],
            out_specs=pl.BlockSpec(memory_space=pl.ANY),
            scratch_shapes=[pltpu.VMEM((3, ROW_BLOCK, d), F32), pltpu.SemaphoreType.DMA((3,)),
                            pltpu.SemaphoreType.DMA((2,))],
        ),
        input_output_aliases={2: 0},
        compiler_params=_cparams(("arbitrary",)),
        name="moe_dispatch",
    )(dest_flat, tokens, buf_prev)


def _expert_kernel(be_ref, nused_ref, x_ref, wgu_ref, wd_ref, o_ref, wgu_bf_ref, wd_bf_ref):
    i = pl.program_id(0)

    @pl.when((i == 0) | (be_ref[i] != be_ref[jnp.maximum(i - 1, 0)]))
    def _():
        wgu_bf_ref[...] = wgu_ref[...].astype(BF16)
        wd_bf_ref[...] = wd_ref[...].astype(BF16)

    @pl.when(i < nused_ref[0])
    def _():
        gu = _dot(x_ref[...].astype(BF16), wgu_bf_ref[...])
        act = _silu(gu[:, :MOE_HIDDEN]) * gu[:, MOE_HIDDEN:]
        o_ref[...] = _dot(act.astype(BF16), wd_bf_ref[...])

    @pl.when(i >= nused_ref[0])
    def _():
        o_ref[...] = jnp.zeros(o_ref.shape, F32)


def _experts(block_e, n_used, buf, w_gu, w_d, layer):
    n_rows, d = buf.shape
    return pl.pallas_call(
        _expert_kernel,
        out_shape=jax.ShapeDtypeStruct((n_rows, d), F32),
        grid_spec=pltpu.PrefetchScalarGridSpec(
            num_scalar_prefetch=2,
            grid=(n_rows // EXPERT_BLOCK,),
            in_specs=[pl.BlockSpec((EXPERT_BLOCK, d), lambda i, be, nu: (i, 0)),
                      pl.BlockSpec((None, None, d, 2 * MOE_HIDDEN), lambda i, be, nu: (layer, be[i], 0, 0)),
                      pl.BlockSpec((None, None, MOE_HIDDEN, d), lambda i, be, nu: (layer, be[i], 0, 0))],
            out_specs=pl.BlockSpec((EXPERT_BLOCK, d), lambda i, be, nu: (i, 0)),
            scratch_shapes=[pltpu.VMEM((d, 2 * MOE_HIDDEN), BF16), pltpu.VMEM((MOE_HIDDEN, d), BF16)],
        ),
        compiler_params=_cparams(("arbitrary",)),
        name="moe_experts",
    )(block_e, n_used, buf, w_gu, w_d)


def _combine_kernel(dest_ref, eo_ref, meta_ref, x_ref, mod_ref, lng_ref, lnb_ref, out_ref,
                    ga0_ref, ga1_ref, gb0_ref, gb1_ref, sem):
    nblk = pl.num_programs(1)
    f = pl.program_id(0) * nblk + pl.program_id(1)
    n = pl.num_programs(0) * nblk
    pairs = ((ga0_ref, ga1_ref), (gb0_ref, gb1_ref))
    m = mod_ref[0]

    def wait_pair(p):
        _for_block_rows(lambda r, k: _row_copy(eo_ref, 0, pairs[p][k], 0, sem.at[p]).wait())

    @pl.when(f == 0)
    def _():
        _for_block_rows(lambda r, k: _row_copy(eo_ref, dest_ref[r * MOE_TOP_K + k], pairs[0][k], r,
                                               sem.at[0]).start(priority=k))

    nxt_base = jnp.minimum(f + 1, n - 1) * ROW_BLOCK

    def step(p):
        cur, nxt = pairs[p], pairs[1 - p]
        wait_pair(p)
        for t in range(ROW_BLOCK // COMBINE_TILE):
            rows = slice(t * COMBINE_TILE, (t + 1) * COMBINE_TILE)
            for r in range(t * COMBINE_TILE, (t + 1) * COMBINE_TILE):
                for k in range(MOE_TOP_K):
                    _row_copy(eo_ref, dest_ref[(nxt_base + r) * MOE_TOP_K + k], nxt[k], r,
                              sem.at[1 - p]).start(priority=k)
            meta = meta_ref[rows, :]
            ffn = (cur[0][rows, :] * meta[:, _META_W:_META_W + 1]
                   + cur[1][rows, :] * meta[:, _META_W + 1:_META_W + 2])
            out_ref[0, rows, :] = _layer_norm(DEEPNORM_ALPHA * x_ref[0, rows, :] + m[5:6, :] * ffn,
                                              lng_ref[...], lnb_ref[...])

        @pl.when(f == n - 1)
        def _():
            wait_pair(1 - p)

    for p in range(2):
        pl.when(f % 2 == p)(functools.partial(step, p))


def _combine(dest_flat, eo, meta, xn, mods, ln_g, ln_b, bsz, drop_rows):
    _, lt, d = xn.shape
    nblk = lt // ROW_BLOCK
    drop = drop_rows // ROW_BLOCK
    return pl.pallas_call(
        _combine_kernel,
        out_shape=jax.ShapeDtypeStruct((bsz, lt - drop_rows, d), F32),
        grid_spec=pltpu.PrefetchScalarGridSpec(
            num_scalar_prefetch=1,
            grid=(bsz, nblk),
            in_specs=[pl.BlockSpec(memory_space=pl.ANY),
                      pl.BlockSpec((ROW_BLOCK, LANES), lambda b, i, dest: (b * nblk + i, 0)),
                      pl.BlockSpec((1, ROW_BLOCK, d), lambda b, i, dest: (b, i, 0)),
                      pl.BlockSpec((1, 6, d), lambda b, i, dest: (jnp.where(i == 0, bsz, b), 0, 0)),
                      pl.BlockSpec((1, d), lambda b, i, dest: (0, 0)),
                      pl.BlockSpec((1, d), lambda b, i, dest: (0, 0))],
            out_specs=pl.BlockSpec((1, ROW_BLOCK, d), lambda b, i, dest: (b, jnp.maximum(i - drop, 0), 0)),
            scratch_shapes=[pltpu.VMEM((ROW_BLOCK, d), F32) for _ in range(2 * MOE_TOP_K)]
            + [pltpu.SemaphoreType.DMA((2,))],
        ),
        compiler_params=_cparams(("arbitrary", "arbitrary")),
        name="moe_combine",
    )(dest_flat, eo, meta, xn, mods, ln_g, ln_b)


def _moe_rows(n_tokens):
    return (MOE_TOP_K * n_tokens + MOE_EXPERTS * (EXPERT_BLOCK - 1)) // EXPERT_BLOCK * EXPERT_BLOCK


def _moe(tokens3, xn, mods, w_r, b_r, w_gu, w_d, layer, ln_g, ln_b, bsz, buf_prev, drop_rows):
    _, lt, d = tokens3.shape
    tokens = tokens3.reshape(bsz * lt, d)
    n_blocks = buf_prev.shape[0] // EXPERT_BLOCK
    meta, cnt = _router(tokens, w_r, b_r)
    eid = meta[:, _META_E:_META_E + MOE_TOP_K].astype(jnp.int32)
    rank = meta[:, _META_RANK:_META_RANK + MOE_TOP_K].astype(jnp.int32)
    counts = cnt[0, :MOE_EXPERTS].astype(jnp.int32)
    padded = (counts + EXPERT_BLOCK - 1) // EXPERT_BLOCK * EXPERT_BLOCK
    pends = jnp.cumsum(padded)
    experts = jnp.arange(MOE_EXPERTS, dtype=jnp.int32)
    run_start = jnp.sum(jnp.where(eid[..., None] == experts, pends - padded, 0), axis=-1)
    dest_flat = (run_start + rank).reshape(-1)
    block_start = jnp.arange(n_blocks, dtype=jnp.int32) * EXPERT_BLOCK
    block_e = jnp.minimum(jnp.sum((pends[None, :] <= block_start[:, None]).astype(jnp.int32), axis=1),
                          MOE_EXPERTS - 1)
    n_used = (pends[-1:] // EXPERT_BLOCK).astype(jnp.int32)
    buf = _dispatch(dest_flat, tokens, buf_prev)
    eo = _experts(block_e, n_used, buf, w_gu, w_d, layer)
    return _combine(dest_flat, eo, meta, xn, mods, ln_g, ln_b, bsz, drop_rows), buf


def kernel(x, c, ctx, c_ctx, mod_w, mod_b, ssd_in_w, ssd_conv_w, ssd_conv_b, ssd_dt_bias, ssd_a_log, ssd_d_skip, ssd_norm_w, ssd_out_w, ret_in_w, ret_decay_logit, ret_gn_w, ret_gn_b, ret_out_w, ln_mix_g, ln_mix_b, ln_ffn_g, ln_ffn_b, moe_group_w, moe_group_b, moe_expert_w, moe_expert_b, moe_w_gate_up, moe_w_down):
    bsz, seqlen, d = x.shape
    ctx_len = ctx.shape[1]
    assert d == D_MODEL and ctx_len == ROW_BLOCK and seqlen % ROW_BLOCK == 0 and seqlen % GRID_W == 0
    assert bsz + 1 <= 16
    depth = mod_w.shape[0]

    xc = jnp.concatenate([ctx, x], axis=1)
    cc = jnp.zeros((16, d), F32).at[:bsz].set(c).at[bsz].set(c_ctx)
    cos, sin = _rope_tables(seqlen // GRID_W, ctx_len)
    w_gu, w_d = moe_w_gate_up, moe_w_down
    buf = jnp.zeros((_moe_rows(bsz * (ctx_len + seqlen)), d), F32)

    for i in range(depth):
        j = i // 2
        mods = _modulation(cc, mod_w, mod_b, i)
        if i % 2 == 0:
            w = ssd_in_w[j]
            n_main = SSD_D_INNER + SSD_CONV_DIM
            w_main = w[:, :n_main].astype(BF16)
            w_dt = jnp.concatenate([w[:, n_main:], w[:, n_main:]], axis=1).astype(BF16)
            dtb = jnp.tile(ssd_dt_bias[j].reshape(1, -1), (1, 2))
            alog = jnp.tile(ssd_a_log[j].reshape(1, -1), (1, 2))
            z, xbc, dtl = _ssd_proj(xc, mods, w_main, w_dt, dtb, alog, ssd_conv_w[j],
                                    ssd_conv_b[j].reshape(1, -1), bsz)
            y_f = _ssd_scan(xbc, dtl, 0, ctx_len)
            dskip = jnp.repeat(ssd_d_skip[j], SSD_HEADDIM).reshape(1, -1)
            y = _ssd_scan(xbc, dtl, 1, ctx_len, z=z, y_f=y_f, dskip=dskip,
                          norm_w=ssd_norm_w[j].reshape(1, -1))
            out_w = ssd_out_w[j]
        else:
            q, k, v, g = _ret_proj(xc, mods, ret_in_w[j].astype(BF16), cos, sin, bsz)
            tabs = _ret_tables(ret_decay_logit[j])
            o_f = _ret_scan(q, k, v, tabs[0], 0, ctx_len)
            y = _ret_scan(q, k, v, tabs[1], 1, ctx_len, g=g, o_f=o_f,
                          gn_w=ret_gn_w[j].reshape(1, -1), gn_b=ret_gn_b[j].reshape(1, -1))
            out_w = ret_out_w[j]
        xn, tokens = _out_proj(y, xc, mods, out_w.astype(BF16), ln_mix_g[i].reshape(1, -1),
                               ln_mix_b[i].reshape(1, -1), bsz)
        w_r = jnp.zeros((d, LANES), F32).at[:, :MOE_GROUPS].set(moe_group_w[i])
        w_r = w_r.at[:, MOE_GROUPS:MOE_GROUPS + MOE_EXPERTS].set(moe_expert_w[i])
        b_r = jnp.zeros((1, LANES), F32).at[0, :MOE_GROUPS].set(moe_group_b[i])
        b_r = b_r.at[0, MOE_GROUPS:MOE_GROUPS + MOE_EXPERTS].set(moe_expert_b[i])
        xc, buf = _moe(tokens, xn, mods, w_r, b_r, w_gu, w_d, i, ln_ffn_g[i].reshape(1, -1),
                       ln_ffn_b[i].reshape(1, -1), bsz, buf, ctx_len if i == depth - 1 else 0)
    return xc
```

```python
import functools
import math

import jax
import jax.numpy as jnp
from jax import lax
from jax.experimental import pallas as pl
from jax.experimental.pallas import tpu as pltpu

F32 = jnp.float32
BF16 = jnp.bfloat16

D_MODEL = 1024
DEPTH = 4
GRID_W = 64
DEEPNORM_ALPHA = (2.0 * DEPTH) ** 0.25
LN_EPS = 1e-5

SSD_D_INNER = 2 * D_MODEL
SSD_HEADDIM = 64
SSD_HEADS = SSD_D_INNER // SSD_HEADDIM
SSD_GROUPS = 4
SSD_HPG = SSD_HEADS // SSD_GROUPS
SSD_STATE = 128
SSD_CONV_W = 5
SSD_BC_DIM = SSD_GROUPS * SSD_STATE
SSD_CONV_DIM = SSD_D_INNER + 2 * SSD_BC_DIM
SSD_GROUP_W = SSD_HPG * SSD_HEADDIM

RET_HEADS = D_MODEL // 256
RET_QK_DIM = D_MODEL // RET_HEADS
RET_VALUE = 2 * D_MODEL
RET_V_DIM = RET_VALUE // RET_HEADS
ROPE_BASE = 10000.0

MOE_GROUPS = 4
MOE_EXPERTS_PER_GROUP = 8
MOE_EXPERTS = MOE_GROUPS * MOE_EXPERTS_PER_GROUP
MOE_HIDDEN = D_MODEL // 2
MOE_TOP_K = 2

CHUNK = 128
LANES = 128
ROW_BLOCK = 256
EXPERT_BLOCK = 512
SCAN_SUB = ROW_BLOCK // CHUNK
CONV_HALO = 8
CONV_COLS = 256
ROW_DMA_UNROLL = 8
COMBINE_TILE = 32
ROUTER_BLOCK = 512
VMEM_LIMIT = 48 * 1024 * 1024
NEG_BIG = -1e30


def _cparams(sem):
    return pltpu.CompilerParams(dimension_semantics=sem, vmem_limit_bytes=VMEM_LIMIT)


def _dot(a, b):
    return jnp.dot(a, b, preferred_element_type=F32)


def _split2(x):
    hi = x.astype(BF16)
    lo = (x - hi.astype(F32)).astype(BF16)
    return hi, lo


def _split3(x):
    hi = x.astype(BF16)
    r = x - hi.astype(F32)
    mid = r.astype(BF16)
    lo = (r - mid.astype(F32)).astype(BF16)
    return hi, mid, lo


def _dot_hi(a, b):
    ah, al = _split2(a)
    bh, bl = _split2(b)
    return _dot(ah, bh) + (_dot(ah, bl) + _dot(al, bh))


def _silu(x):
    return x * jax.nn.sigmoid(x)


def _softplus(x):
    return jnp.maximum(x, 0.0) + jnp.log1p(jnp.exp(-jnp.abs(x)))


def _mod_kernel(c_ref, w_ref, b_ref, o_ref):
    o_ref[...] = _dot_hi(_silu(c_ref[...]), w_ref[...]) + b_ref[...]


def _modulation(cc, mod_w, mod_b, layer):
    rows, d = cc.shape
    n = mod_w.shape[-1]
    tn = 1024
    out = pl.pallas_call(
        _mod_kernel,
        out_shape=jax.ShapeDtypeStruct((rows, n), F32),
        grid=(n // tn,),
        in_specs=[pl.BlockSpec((rows, d), lambda j: (0, 0)),
                  pl.BlockSpec((None, d, tn), lambda j: (layer, 0, j)),
                  pl.BlockSpec((None, 1, tn), lambda j: (layer, 0, j))],
        out_specs=pl.BlockSpec((rows, tn), lambda j: (0, j)),
        compiler_params=_cparams(("parallel",)),
        name="modulation",
    )(cc, mod_w, mod_b.reshape(mod_b.shape[0], 1, n))
    return out.reshape(rows, 6, d)


def _mod_spec(bsz):
    return pl.BlockSpec((1, 6, D_MODEL), lambda b, i: (jnp.where(i == 0, bsz, b), 0, 0))


def _ssd_proj_kernel(x_ref, xlo_ref, xhi_ref, mod_ref, w_ref, wdt_ref, dtb_ref, alog_ref, cw_ref, cb_ref,
                     z_ref, xbc_ref, dtl_ref):
    i = pl.program_id(1)
    m = mod_ref[0]
    main = slice(CONV_HALO, CONV_HALO + ROW_BLOCK)
    n_all = ROW_BLOCK + 2 * CONV_HALO
    x_all = jnp.concatenate([xlo_ref[0], x_ref[0], xhi_ref[0]], axis=0)
    u = (x_all * (1.0 + m[1:2, :]) + m[0:1, :]).astype(BF16)
    lo_ok = jnp.where(i >= 2, 1.0, 0.0)
    hi_ok = jnp.where((i >= 1) & (i < pl.num_programs(1) - 1), 1.0, 0.0)
    half = SSD_CONV_W // 2
    for c0 in range(0, SSD_CONV_DIM, CONV_COLS):
        cs = slice(c0, c0 + CONV_COLS)
        v = _dot(u, w_ref[:, SSD_D_INNER + c0:SSD_D_INNER + c0 + CONV_COLS])
        v = jnp.concatenate([v[0:CONV_HALO] * lo_ok, v[main], v[CONV_HALO + ROW_BLOCK:n_all] * hi_ok], axis=0)
        if c0 < SSD_D_INNER:
            z_ref[0, :, cs] = _dot(u, w_ref[:, cs])[main]
        acc = cb_ref[:, cs] + cw_ref[half:half + 1, cs] * v[main]
        for k in range(SSD_CONV_W):
            if k != half:
                acc = acc + cw_ref[k:k + 1, cs] * pltpu.roll(v, (half - k) % n_all, 0)[main]
        xbc_ref[0, :, cs] = _silu(acc)
    dt = _softplus(_dot(u, wdt_ref[...])[main] + dtb_ref[...])
    lane = lax.broadcasted_iota(jnp.int32, dt.shape, 1)
    dtl_ref[0] = jnp.where(lane < 2 * SSD_HEADS, dt, dt * (-jnp.exp(alog_ref[...])))


def _ssd_proj(xc, mods, w_main, w_dt, dtb, alog, conv_w, conv_b, bsz):
    _, lt, d = xc.shape
    nblk = lt // ROW_BLOCK
    per = ROW_BLOCK // CONV_HALO
    row = lambda w: pl.BlockSpec((1, ROW_BLOCK, w), lambda b, i: (b, i, 0))
    full = lambda a: pl.BlockSpec(a.shape, lambda b, i: (0,) * a.ndim)
    lo = pl.BlockSpec((1, CONV_HALO, d), lambda b, i: (b, jnp.maximum(i * per - 1, 0), 0))
    hi = pl.BlockSpec((1, CONV_HALO, d), lambda b, i: (b, jnp.minimum((i + 1) * per, nblk * per - 1), 0))
    return pl.pallas_call(
        _ssd_proj_kernel,
        out_shape=(jax.ShapeDtypeStruct((bsz, lt, SSD_D_INNER), F32),
                   jax.ShapeDtypeStruct((bsz, lt, SSD_CONV_DIM), F32),
                   jax.ShapeDtypeStruct((bsz, lt, LANES), F32)),
        grid=(bsz, nblk),
        in_specs=[row(d), lo, hi, _mod_spec(bsz), full(w_main), full(w_dt), full(dtb), full(alog),
                  full(conv_w), full(conv_b)],
        out_specs=(row(SSD_D_INNER), row(SSD_CONV_DIM), row(LANES)),
        compiler_params=_cparams(("parallel", "parallel")),
        name="ssd_in_proj",
    )(xc, xc, xc, mods, w_main, w_dt, dtb, alog, conv_w, conv_b)


def _scan_block_index(direction, ctx_blocks, n_blocks):
    if direction == 0:
        return lambda j: j
    return lambda j: jnp.where(j < ctx_blocks, ctx_blocks - 1 - j, n_blocks - 1 + ctx_blocks - j)


def _scan_rows(direction, s):
    c = s if direction == 0 else SCAN_SUB - 1 - s
    return slice(c * CHUNK, (c + 1) * CHUNK)


def _ssd_scan_kernel(*refs, direction, final):
    if final:
        xs_ref, b_ref, c_ref, dtl_ref, z_ref, yf_ref, dskip_ref, nw_ref, out_ref, h_ref, y_ref = refs
    else:
        xs_ref, b_ref, c_ref, dtl_ref, out_ref, h_ref = refs

    @pl.when(pl.program_id(1) == 0)
    def _():
        h_ref[...] = jnp.zeros(h_ref.shape, F32)

    row = lax.broadcasted_iota(jnp.int32, (CHUNK, CHUNK), 0)
    col = lax.broadcasted_iota(jnp.int32, (CHUNK, CHUNK), 1)
    if direction == 0:
        mask, end = row >= col, CHUNK - 1
    else:
        mask, end = col >= row, 0
    tri = jnp.where(mask, 1.0, 0.0).astype(BF16)
    tri_t = jnp.where(mask, 0.0, 1.0)
    tri_t = jnp.where(row == col, 1.0, tri_t).astype(BF16)
    lane_lo = col < SSD_HEADDIM
    o_dt = SSD_HEADS * direction
    o_la = 2 * SSD_HEADS + SSD_HEADS * direction

    for s in range(SCAN_SUB):
        rows = _scan_rows(direction, s)
        dtl = dtl_ref[0, rows, :]
        dtl_t = dtl.T
        p0, p1, p2 = _split3(dtl)
        cum = _dot(tri, p0) + _dot(tri, p1) + _dot(tri, p2)
        q0, q1, q2 = _split3(dtl_t)
        cum_t = _dot(q0, tri_t) + _dot(q1, tri_t) + _dot(q2, tri_t)
        dt_t = dtl_t[o_dt:o_dt + SSD_HEADS, :]
        a_t = cum_t[o_la:o_la + SSD_HEADS, :]
        w1_t = dt_t * jnp.exp(a_t[:, end:end + 1] - a_t)
        a_dt_t = a_t - jnp.log(dt_t)

        for g in range(SSD_GROUPS):
            gs = slice(g * SSD_STATE, (g + 1) * SSD_STATE)
            bg = b_ref[0, rows, gs]
            cg = c_ref[0, rows, gs].astype(BF16)
            cb = lax.dot_general(cg, bg.astype(BF16), (((1,), (1,)), ((), ())), preferred_element_type=F32)
            bg_t = bg.T
            h_in = h_ref[g]
            y_off = _dot(cg, h_in.astype(BF16))
            for jp in range(SSD_HPG // 2):
                ls = slice(g * SSD_GROUP_W + jp * LANES, g * SSD_GROUP_W + (jp + 1) * LANES)
                gl = slice(jp * LANES, (jp + 1) * LANES)
                xp = xs_ref[0, rows, ls]
                top, bot, acols = [], [], []
                for e in range(2):
                    h = g * SSD_HPG + 2 * jp + e
                    a_col = jnp.broadcast_to(cum[:, o_la + h:o_la + h + 1], (CHUNK, CHUNK))
                    seg = jnp.where(mask, a_col - a_dt_t[h:h + 1, :], NEG_BIG)
                    top.append(cb * jnp.exp(seg))
                    bot.append(bg_t * w1_t[h:h + 1, :])
                    acols.append(a_col)
                lhs = jnp.concatenate([jnp.concatenate(top, axis=1), jnp.concatenate(bot, axis=1)], axis=0)
                rhs = jnp.concatenate([jnp.where(lane_lo, xp, 0.0), jnp.where(lane_lo, 0.0, xp)], axis=0)
                res = _dot(lhs.astype(BF16), rhs.astype(BF16))
                ea = jnp.exp(jnp.where(lane_lo, acols[0], acols[1]))
                y_pair = res[0:CHUNK] + y_off[:, gl] * ea
                if final:
                    y_ref[:, ls] = y_pair
                else:
                    out_ref[0, rows, ls] = y_pair
                h_ref[g, :, gl] = h_in[:, gl] * ea[end:end + 1, :] + res[CHUNK:2 * CHUNK]

        if final:
            y = y_ref[...] + yf_ref[0, rows, :] + xs_ref[0, rows, :] * dskip_ref[...]
            y = y * _silu(z_ref[0, rows, :])
            gw = SSD_D_INNER // SSD_GROUPS
            for g in range(SSD_GROUPS):
                gs = slice(g * gw, (g + 1) * gw)
                yg = y[:, gs]
                ms = jnp.mean(yg * yg, axis=-1, keepdims=True)
                out_ref[0, rows, gs] = (yg * lax.rsqrt(ms + LN_EPS) * nw_ref[:, gs]).astype(out_ref.dtype)


def _ssd_scan(xbc, dtl, direction, ctx_len, z=None, y_f=None, dskip=None, norm_w=None):
    bsz, lt, _ = xbc.shape
    bi = _scan_block_index(direction, ctx_len // ROW_BLOCK, lt // ROW_BLOCK)
    final = z is not None
    col = lambda w, cblk: pl.BlockSpec((1, ROW_BLOCK, w), lambda b, j: (b, bi(j), cblk))
    in_specs = [col(SSD_D_INNER, 0), col(SSD_BC_DIM, SSD_D_INNER // SSD_BC_DIM),
                col(SSD_BC_DIM, SSD_D_INNER // SSD_BC_DIM + 1), col(LANES, 0)]
    args = [xbc, xbc, xbc, dtl]
    scratch = [pltpu.VMEM((SSD_GROUPS, SSD_STATE, SSD_GROUP_W), F32)]
    if final:
        vec = pl.BlockSpec((1, SSD_D_INNER), lambda b, j: (0, 0))
        in_specs += [col(SSD_D_INNER, 0), col(SSD_D_INNER, 0), vec, vec]
        args += [z, y_f, dskip, norm_w]
        scratch.append(pltpu.VMEM((CHUNK, SSD_D_INNER), F32))
    return pl.pallas_call(
        functools.partial(_ssd_scan_kernel, direction=direction, final=final),
        out_shape=jax.ShapeDtypeStruct((bsz, lt, SSD_D_INNER), BF16 if final else F32),
        grid=(bsz, lt // ROW_BLOCK),
        in_specs=in_specs,
        out_specs=col(SSD_D_INNER, 0),
        scratch_shapes=scratch,
        compiler_params=_cparams(("parallel", "arbitrary")),
        name="ssd_scan_bwd" if final else "ssd_scan_fwd",
    )(*args)


def _ret_proj_kernel(x_ref, mod_ref, w_ref, cos_ref, sin_ref, q_ref, k_ref, v_ref, g_ref):
    m = mod_ref[0]
    u = (x_ref[0] * (1.0 + m[1:2, :]) + m[0:1, :]).astype(BF16)
    cs, sn = cos_ref[...], sin_ref[...]
    half = RET_QK_DIM // 2

    def rope(t, out_ref):
        for h in range(RET_HEADS):
            t1 = t[:, h * RET_QK_DIM:h * RET_QK_DIM + half]
            t2 = t[:, h * RET_QK_DIM + half:(h + 1) * RET_QK_DIM]
            out_ref[0, :, h * RET_QK_DIM:h * RET_QK_DIM + half] = t1 * cs - t2 * sn
            out_ref[0, :, h * RET_QK_DIM + half:(h + 1) * RET_QK_DIM] = t1 * sn + t2 * cs

    rope(_dot(u, w_ref[:, 0:D_MODEL]), q_ref)
    rope(_dot(u, w_ref[:, D_MODEL:2 * D_MODEL]) * (RET_QK_DIM ** -0.5), k_ref)
    v_ref[0] = _dot(u, w_ref[:, 2 * D_MODEL:2 * D_MODEL + RET_VALUE])
    g_ref[0] = _dot(u, w_ref[:, 2 * D_MODEL + RET_VALUE:2 * D_MODEL + 2 * RET_VALUE])


def _ret_proj(xc, mods, w, cos, sin, bsz):
    _, lt, d = xc.shape
    row = lambda wd: pl.BlockSpec((1, ROW_BLOCK, wd), lambda b, i: (b, i, 0))
    tab = pl.BlockSpec((ROW_BLOCK, RET_QK_DIM // 2), lambda b, i: (i, 0))
    return pl.pallas_call(
        _ret_proj_kernel,
        out_shape=(jax.ShapeDtypeStruct((bsz, lt, D_MODEL), F32),
                   jax.ShapeDtypeStruct((bsz, lt, D_MODEL), F32),
                   jax.ShapeDtypeStruct((bsz, lt, RET_VALUE), F32),
                   jax.ShapeDtypeStruct((bsz, lt, RET_VALUE), F32)),
        grid=(bsz, lt // ROW_BLOCK),
        in_specs=[row(d), _mod_spec(bsz), pl.BlockSpec(w.shape, lambda b, i: (0, 0)), tab, tab],
        out_specs=(row(D_MODEL), row(D_MODEL), row(RET_VALUE), row(RET_VALUE)),
        compiler_params=_cparams(("parallel", "parallel")),
        name="ret_in_proj",
    )(xc, mods, w, cos, sin)


def _ret_scan_kernel(*refs, direction, final):
    if final:
        (q_ref, k_ref, v_ref, xi_ref, zeta_ref, dmat_ref, cdec_ref,
         g_ref, of_ref, gnw_ref, gnb_ref, out_ref, s_ref) = refs
    else:
        q_ref, k_ref, v_ref, xi_ref, zeta_ref, dmat_ref, cdec_ref, out_ref, s_ref = refs

    @pl.when(pl.program_id(1) == 0)
    def _():
        s_ref[...] = jnp.zeros(s_ref.shape, F32)

    for s in range(SCAN_SUB):
        rows = _scan_rows(direction, s)
        for h in range(RET_HEADS):
            qs = slice(h * RET_QK_DIM, (h + 1) * RET_QK_DIM)
            vs = slice(h * RET_V_DIM, (h + 1) * RET_V_DIM)
            qh, kh = q_ref[0, rows, qs], k_ref[0, rows, qs]
            vh = v_ref[0, rows, vs].astype(BF16)
            s_in = s_ref[h]
            sc = lax.dot_general(qh.astype(BF16), kh.astype(BF16), (((1,), (1,)), ((), ())),
                                 preferred_element_type=F32) * dmat_ref[h]
            o = _dot(sc.astype(BF16), vh) + _dot((qh * xi_ref[:, qs]).astype(BF16), s_in.astype(BF16))
            kz_t = (kh * zeta_ref[:, qs]).T
            s_ref[h] = s_in * cdec_ref[h] + _dot(kz_t.astype(BF16), vh)
            if final:
                o = o + of_ref[0, rows, vs]
                mu = jnp.mean(o, axis=-1, keepdims=True)
                var = jnp.mean(jnp.square(o - mu), axis=-1, keepdims=True)
                o = (o - mu) * lax.rsqrt(var + LN_EPS) * gnw_ref[:, vs] + gnb_ref[:, vs]
                out_ref[0, rows, vs] = (o * _silu(g_ref[0, rows, vs])).astype(out_ref.dtype)
            else:
                out_ref[0, rows, vs] = o


def _ret_scan(q, k, v, tabs, direction, ctx_len, g=None, o_f=None, gn_w=None, gn_b=None):
    bsz, lt, _ = q.shape
    bi = _scan_block_index(direction, ctx_len // ROW_BLOCK, lt // ROW_BLOCK)
    final = g is not None
    xi, zeta, dmat, cdec = tabs
    col = lambda w: pl.BlockSpec((1, ROW_BLOCK, w), lambda b, j: (b, bi(j), 0))
    full = lambda a: pl.BlockSpec(a.shape, lambda b, j: (0,) * a.ndim)
    in_specs = [col(D_MODEL), col(D_MODEL), col(RET_VALUE), full(xi), full(zeta), full(dmat), full(cdec)]
    args = [q, k, v, xi, zeta, dmat, cdec]
    if final:
        vec = pl.BlockSpec((1, RET_VALUE), lambda b, j: (0, 0))
        in_specs += [col(RET_VALUE), col(RET_VALUE), vec, vec]
        args += [g, o_f, gn_w, gn_b]
    return pl.pallas_call(
        functools.partial(_ret_scan_kernel, direction=direction, final=final),
        out_shape=jax.ShapeDtypeStruct((bsz, lt, RET_VALUE), BF16 if final else F32),
        grid=(bsz, lt // ROW_BLOCK),
        in_specs=in_specs,
        out_specs=col(RET_VALUE),
        scratch_shapes=[pltpu.VMEM((RET_HEADS, RET_QK_DIM, RET_V_DIM), F32)],
        compiler_params=_cparams(("parallel", "arbitrary")),
        name="ret_scan_bwd" if final else "ret_scan_fwd",
    )(*args)


def _ret_tables(decay_logit):
    lg = jax.nn.log_sigmoid(decay_logit.astype(F32))
    pos = jnp.arange(CHUNK, dtype=F32)
    rel = pos[:, None] - pos[None, :]
    out = []
    for d in range(2):
        l = lg[d]
        if d == 0:
            xi_e, zeta_e, r = pos + 1.0, CHUNK - 1.0 - pos, rel
        else:
            xi_e, zeta_e, r = CHUNK - pos, pos, -rel
        xi = jnp.repeat(jnp.exp(xi_e[:, None] * l), RET_QK_DIM, axis=1)
        zeta = jnp.repeat(jnp.exp(zeta_e[:, None] * l), RET_QK_DIM, axis=1)
        dmat = jnp.exp(jnp.where((r >= 0)[None], r[None] * l[:, None, None], -jnp.inf))
        cdec = jnp.broadcast_to(jnp.exp(CHUNK * l)[:, None, None], (RET_HEADS, 1, RET_V_DIM))
        out.append((xi, zeta, dmat, cdec))
    return out


def _rope_tables(n_rows, ctx_len):
    rows, cols = jnp.meshgrid(jnp.arange(n_rows), jnp.arange(GRID_W), indexing='ij')
    rows = rows.reshape(-1).astype(F32)
    cols = cols.reshape(-1).astype(F32)
    n_freq = RET_QK_DIM // 4
    inv_freq = ROPE_BASE ** (-jnp.arange(n_freq, dtype=F32) / n_freq)
    ang = jnp.concatenate([rows[:, None] * inv_freq, cols[:, None] * inv_freq], -1)
    cos = jnp.concatenate([jnp.ones((ctx_len, ang.shape[1]), F32), jnp.cos(ang)], 0)
    sin = jnp.concatenate([jnp.zeros((ctx_len, ang.shape[1]), F32), jnp.sin(ang)], 0)
    return cos, sin


def _layer_norm(t, g, b):
    mu = jnp.mean(t, axis=-1, keepdims=True)
    var = jnp.mean(jnp.square(t - mu), axis=-1, keepdims=True)
    return (t - mu) * lax.rsqrt(var + LN_EPS) * g + b


def _out_kernel(y_ref, x_ref, mod_ref, w_ref, lng_ref, lnb_ref, xn_ref, tok_ref):
    m = mod_ref[0]
    o = _dot(y_ref[0], w_ref[...])
    xn = _layer_norm(DEEPNORM_ALPHA * x_ref[0] + m[2:3, :] * o, lng_ref[...], lnb_ref[...])
    xn_ref[0] = xn
    tok_ref[0] = xn * (1.0 + m[4:5, :]) + m[3:4, :]


def _out_proj(y, xc, mods, w, ln_g, ln_b, bsz):
    _, lt, d = xc.shape
    row = lambda wd: pl.BlockSpec((1, ROW_BLOCK, wd), lambda b, i: (b, i, 0))
    vec = pl.BlockSpec((1, d), lambda b, i: (0, 0))
    return pl.pallas_call(
        _out_kernel,
        out_shape=(jax.ShapeDtypeStruct((bsz, lt, d), F32), jax.ShapeDtypeStruct((bsz, lt, d), F32)),
        grid=(bsz, lt // ROW_BLOCK),
        in_specs=[row(y.shape[-1]), row(d), _mod_spec(bsz), pl.BlockSpec(w.shape, lambda b, i: (0, 0)),
                  vec, vec],
        out_specs=(row(d), row(d)),
        compiler_params=_cparams(("parallel", "parallel")),
        name="out_proj_norm",
    )(y, xc, mods, w, ln_g, ln_b)


_META_E, _META_RANK, _META_W = 0, 2, 4


def _router_kernel(tok_ref, w_ref, b_ref, meta_ref, cnt_ref, carry_ref):
    @pl.when(pl.program_id(0) == 0)
    def _():
        carry_ref[...] = jnp.zeros(carry_ref.shape, F32)

    th, tl = _split2(tok_ref[...])
    wh, wl = _split2(w_ref[...])
    both = _dot(th, jnp.concatenate([wh, wl], axis=1))
    logits = both[:, :LANES] + (both[:, LANES:] + _dot(tl, wh)) + b_ref[...]
    shape = logits.shape
    lane = lax.broadcasted_iota(jnp.int32, shape, 1).astype(F32)
    big = float(LANES)

    def first_max(vals):
        m = jnp.max(vals, axis=-1, keepdims=True)
        return m, jnp.min(jnp.where(vals == m, lane, big), axis=-1, keepdims=True)

    gl = jnp.where(lane < MOE_GROUPS, logits, -jnp.inf)
    gmax, gsel = first_max(gl)
    g_gate = 1.0 / jnp.sum(jnp.exp(gl - gmax), axis=-1, keepdims=True)
    lo = MOE_GROUPS + gsel * MOE_EXPERTS_PER_GROUP
    el = jnp.where((lane >= lo) & (lane < lo + MOE_EXPERTS_PER_GROUP), logits, -jnp.inf)
    m1, i1 = first_max(el)
    m2, i2 = first_max(jnp.where(lane == i1, -jnp.inf, el))
    e2 = jnp.exp(m2 - m1)
    w1 = g_gate / (1.0 + e2)
    w2 = g_gate * e2 / (1.0 + e2)
    e1, e2id = i1 - MOE_GROUPS, i2 - MOE_GROUPS

    oh1 = jnp.where(lane == e1, 1.0, 0.0)
    oh2 = jnp.where(lane == e2id, 1.0, 0.0)
    ohs = oh1 + oh2
    n = shape[0]
    r = lax.broadcasted_iota(jnp.int32, (n, n), 0)
    c = lax.broadcasted_iota(jnp.int32, (n, n), 1)
    before = _dot(jnp.where(c < r, 1.0, 0.0).astype(BF16), ohs.astype(BF16)) + carry_ref[...]
    rank1 = jnp.sum(oh1 * before, axis=-1, keepdims=True)
    rank2 = jnp.sum(oh2 * before, axis=-1, keepdims=True)
    carry_ref[...] = carry_ref[...] + jnp.sum(ohs, axis=0, keepdims=True)
    cnt_ref[...] = carry_ref[...]

    rec = jnp.zeros(shape, F32)
    for k, val in enumerate((e1, e2id, rank1, rank2, w1, w2)):
        rec = jnp.where(lane == float(k), val, rec)
    meta_ref[...] = rec


def _router(tokens, w_r, b_r):
    t, d = tokens.shape
    block = math.gcd(t, ROUTER_BLOCK)
    assert block % ROW_BLOCK == 0
    return pl.pallas_call(
        _router_kernel,
        out_shape=(jax.ShapeDtypeStruct((t, LANES), F32), jax.ShapeDtypeStruct((1, LANES), F32)),
        grid=(t // block,),
        in_specs=[pl.BlockSpec((block, d), lambda i: (i, 0)),
                  pl.BlockSpec((d, LANES), lambda i: (0, 0)),
                  pl.BlockSpec((1, LANES), lambda i: (0, 0))],
        out_specs=(pl.BlockSpec((block, LANES), lambda i: (i, 0)),
                   pl.BlockSpec((1, LANES), lambda i: (0, 0))),
        scratch_shapes=[pltpu.VMEM((1, LANES), F32)],
        compiler_params=_cparams(("arbitrary",)),
        name="moe_router",
    )(tokens, w_r, b_r)


def _row_copy(src_ref, src_row, dst_ref, dst_row, sem):
    return pltpu.make_async_copy(src_ref.at[pl.ds(src_row, 1), :], dst_ref.at[pl.ds(dst_row, 1), :], sem)


def _for_block_rows(fn):
    def body(rb, carry):
        for u in range(ROW_DMA_UNROLL):
            for k in range(MOE_TOP_K):
                fn(rb * ROW_DMA_UNROLL + u, k)
        return carry
    lax.fori_loop(0, ROW_BLOCK // ROW_DMA_UNROLL, body, 0)


def _dispatch_kernel(dest_ref, tok_ref, buf_in_ref, buf_ref, stage_ref, load_sem, row_sem):
    del buf_in_ref
    i = pl.program_id(0)
    n = pl.num_programs(0)

    def load(blk):
        slot = blk % 3
        return pltpu.make_async_copy(tok_ref.at[pl.ds(blk * ROW_BLOCK, ROW_BLOCK), :], stage_ref.at[slot],
                                     load_sem.at[slot])

    @pl.when(i == 0)
    def _():
        load(i).start()

    @pl.when(i + 1 < n)
    def _():
        load(i + 1).start()

    load(i).wait()
    stage = stage_ref.at[i % 3]
    base = i * ROW_BLOCK
    for r in range(ROW_BLOCK):
        for k in range(MOE_TOP_K):
            _row_copy(stage, r, buf_ref, dest_ref[(base + r) * MOE_TOP_K + k],
                      row_sem.at[i % 2]).start(priority=k)

    def drain(blk):
        _for_block_rows(lambda r, k: _row_copy(stage_ref.at[0], 0, buf_ref, 0, row_sem.at[blk % 2]).wait())

    @pl.when(i > 0)
    def _():
        drain(i - 1)

    @pl.when(i == n - 1)
    def _():
        drain(i)


def _dispatch(dest_flat, tokens, buf_prev):
    t, d = tokens.shape
    return pl.pallas_call(
        _dispatch_kernel,
        out_shape=jax.ShapeDtypeStruct(buf_prev.shape, F32),
        grid_spec=pltpu.PrefetchScalarGridSpec(
            num_scalar_prefetch=1,
            grid=(t // ROW_BLOCK,),
            in_specs=[pl.BlockSpec(memory_space=pl.ANY), pl.BlockSpec(memory_space=pl.ANY)],
            out_specs=pl.BlockSpec(memory_space=pl.ANY),
            scratch_shapes=[pltpu.VMEM((3, ROW_BLOCK, d), F32), pltpu.SemaphoreType.DMA((3,)),
                            pltpu.SemaphoreType.DMA((2,))],
        ),
        input_output_aliases={2: 0},
        compiler_params=_cparams(("arbitrary",)),
        name="moe_dispatch",
    )(dest_flat, tokens, buf_prev)


def _expert_kernel(be_ref, nused_ref, x_ref, wgu_ref, wd_ref, o_ref, wgu_bf_ref, wd_bf_ref):
    i = pl.program_id(0)

    @pl.when((i == 0) | (be_ref[i] != be_ref[jnp.maximum(i - 1, 0)]))
    def _():
        wgu_bf_ref[...] = wgu_ref[...].astype(BF16)
        wd_bf_ref[...] = wd_ref[...].astype(BF16)

    @pl.when(i < nused_ref[0])
    def _():
        gu = _dot(x_ref[...].astype(BF16), wgu_bf_ref[...])
        act = _silu(gu[:, :MOE_HIDDEN]) * gu[:, MOE_HIDDEN:]
        o_ref[...] = _dot(act.astype(BF16), wd_bf_ref[...])

    @pl.when(i >= nused_ref[0])
    def _():
        o_ref[...] = jnp.zeros(o_ref.shape, F32)


def _experts(block_e, n_used, buf, w_gu, w_d, layer):
    n_rows, d = buf.shape
    return pl.pallas_call(
        _expert_kernel,
        out_shape=jax.ShapeDtypeStruct((n_rows, d), F32),
        grid_spec=pltpu.PrefetchScalarGridSpec(
            num_scalar_prefetch=2,
            grid=(n_rows // EXPERT_BLOCK,),
            in_specs=[pl.BlockSpec((EXPERT_BLOCK, d), lambda i, be, nu: (i, 0)),
                      pl.BlockSpec((None, None, d, 2 * MOE_HIDDEN), lambda i, be, nu: (layer, be[i], 0, 0)),
                      pl.BlockSpec((None, None, MOE_HIDDEN, d), lambda i, be, nu: (layer, be[i], 0, 0))],
            out_specs=pl.BlockSpec((EXPERT_BLOCK, d), lambda i, be, nu: (i, 0)),
            scratch_shapes=[pltpu.VMEM((d, 2 * MOE_HIDDEN), BF16), pltpu.VMEM((MOE_HIDDEN, d), BF16)],
        ),
        compiler_params=_cparams(("arbitrary",)),
        name="moe_experts",
    )(block_e, n_used, buf, w_gu, w_d)


def _combine_kernel(dest_ref, eo_ref, meta_ref, x_ref, mod_ref, lng_ref, lnb_ref, out_ref,
                    ga0_ref, ga1_ref, gb0_ref, gb1_ref, sem):
    nblk = pl.num_programs(1)
    f = pl.program_id(0) * nblk + pl.program_id(1)
    n = pl.num_programs(0) * nblk
    pairs = ((ga0_ref, ga1_ref), (gb0_ref, gb1_ref))
    m = mod_ref[0]

    def wait_pair(p):
        _for_block_rows(lambda r, k: _row_copy(eo_ref, 0, pairs[p][k], 0, sem.at[p]).wait())

    @pl.when(f == 0)
    def _():
        _for_block_rows(lambda r, k: _row_copy(eo_ref, dest_ref[r * MOE_TOP_K + k], pairs[0][k], r,
                                               sem.at[0]).start(priority=k))

    nxt_base = jnp.minimum(f + 1, n - 1) * ROW_BLOCK

    def step(p):
        cur, nxt = pairs[p], pairs[1 - p]
        wait_pair(p)
        for t in range(ROW_BLOCK // COMBINE_TILE):
            rows = slice(t * COMBINE_TILE, (t + 1) * COMBINE_TILE)
            for r in range(t * COMBINE_TILE, (t + 1) * COMBINE_TILE):
                for k in range(MOE_TOP_K):
                    _row_copy(eo_ref, dest_ref[(nxt_base + r) * MOE_TOP_K + k], nxt[k], r,
                              sem.at[1 - p]).start(priority=k)
            meta = meta_ref[rows, :]
            ffn = (cur[0][rows, :] * meta[:, _META_W:_META_W + 1]
                   + cur[1][rows, :] * meta[:, _META_W + 1:_META_W + 2])
            out_ref[0, rows, :] = _layer_norm(DEEPNORM_ALPHA * x_ref[0, rows, :] + m[5:6, :] * ffn,
                                              lng_ref[...], lnb_ref[...])

        @pl.when(f == n - 1)
        def _():
            wait_pair(1 - p)

    for p in range(2):
        pl.when(f % 2 == p)(functools.partial(step, p))


def _combine(dest_flat, eo, meta, xn, mods, ln_g, ln_b, bsz, drop_rows):
    _, lt, d = xn.shape
    nblk = lt // ROW_BLOCK
    drop = drop_rows // ROW_BLOCK
    return pl.pallas_call(
        _combine_kernel,
        out_shape=jax.ShapeDtypeStruct((bsz, lt - drop_rows, d), F32),
        grid_spec=pltpu.PrefetchScalarGridSpec(
            num_scalar_prefetch=1,
            grid=(bsz, nblk),
            in_specs=[pl.BlockSpec(memory_space=pl.ANY),
                      pl.BlockSpec((ROW_BLOCK, LANES), lambda b, i, dest: (b * nblk + i, 0)),
                      pl.BlockSpec((1, ROW_BLOCK, d), lambda b, i, dest: (b, i, 0)),
                      pl.BlockSpec((1, 6, d), lambda b, i, dest: (jnp.where(i == 0, bsz, b), 0, 0)),
                      pl.BlockSpec((1, d), lambda b, i, dest: (0, 0)),
                      pl.BlockSpec((1, d), lambda b, i, dest: (0, 0))],
            out_specs=pl.BlockSpec((1, ROW_BLOCK, d), lambda b, i, dest: (b, jnp.maximum(i - drop, 0), 0)),
            scratch_shapes=[pltpu.VMEM((ROW_BLOCK, d), F32) for _ in range(2 * MOE_TOP_K)]
            + [pltpu.SemaphoreType.DMA((2,))],
        ),
        compiler_params=_cparams(("arbitrary", "arbitrary")),
        name="moe_combine",
    )(dest_flat, eo, meta, xn, mods, ln_g, ln_b)


def _moe_rows(n_tokens):
    return (MOE_TOP_K * n_tokens + MOE_EXPERTS * (EXPERT_BLOCK - 1)) // EXPERT_BLOCK * EXPERT_BLOCK


def _moe(tokens3, xn, mods, w_r, b_r, w_gu, w_d, layer, ln_g, ln_b, bsz, buf_prev, drop_rows):
    _, lt, d = tokens3.shape
    tokens = tokens3.reshape(bsz * lt, d)
    n_blocks = buf_prev.shape[0] // EXPERT_BLOCK
    meta, cnt = _router(tokens, w_r, b_r)
    eid = meta[:, _META_E:_META_E + MOE_TOP_K].astype(jnp.int32)
    rank = meta[:, _META_RANK:_META_RANK + MOE_TOP_K].astype(jnp.int32)
    counts = cnt[0, :MOE_EXPERTS].astype(jnp.int32)
    padded = (counts + EXPERT_BLOCK - 1) // EXPERT_BLOCK * EXPERT_BLOCK
    pends = jnp.cumsum(padded)
    experts = jnp.arange(MOE_EXPERTS, dtype=jnp.int32)
    run_start = jnp.sum(jnp.where(eid[..., None] == experts, pends - padded, 0), axis=-1)
    dest_flat = (run_start + rank).reshape(-1)
    block_start = jnp.arange(n_blocks, dtype=jnp.int32) * EXPERT_BLOCK
    block_e = jnp.minimum(jnp.sum((pends[None, :] <= block_start[:, None]).astype(jnp.int32), axis=1),
                          MOE_EXPERTS - 1)
    n_used = (pends[-1:] // EXPERT_BLOCK).astype(jnp.int32)
    buf = _dispatch(dest_flat, tokens, buf_prev)
    eo = _experts(block_e, n_used, buf, w_gu, w_d, layer)
    return _combine(dest_flat, eo, meta, xn, mods, ln_g, ln_b, bsz, drop_rows), buf


def kernel(x, c, ctx, c_ctx, mod_w, mod_b, ssd_in_w, ssd_conv_w, ssd_conv_b, ssd_dt_bias, ssd_a_log, ssd_d_skip, ssd_norm_w, ssd_out_w, ret_in_w, ret_decay_logit, ret_gn_w, ret_gn_b, ret_out_w, ln_mix_g, ln_mix_b, ln_ffn_g, ln_ffn_b, moe_group_w, moe_group_b, moe_expert_w, moe_expert_b, moe_w_gate_up, moe_w_down):
    bsz, seqlen, d = x.shape
    ctx_len = ctx.shape[1]
    assert d == D_MODEL and ctx_len == ROW_BLOCK and seqlen % ROW_BLOCK == 0 and seqlen % GRID_W == 0
    assert bsz + 1 <= 16
    depth = mod_w.shape[0]

    xc = jnp.concatenate([ctx, x], axis=1)
    cc = jnp.zeros((16, d), F32).at[:bsz].set(c).at[bsz].set(c_ctx)
    cos, sin = _rope_tables(seqlen // GRID_W, ctx_len)
    w_gu, w_d = moe_w_gate_up, moe_w_down
    buf = jnp.zeros((_moe_rows(bsz * (ctx_len + seqlen)), d), F32)

    for i in range(depth):
        j = i // 2
        mods = _modulation(cc, mod_w, mod_b, i)
        if i % 2 == 0:
            w = ssd_in_w[j]
            n_main = SSD_D_INNER + SSD_CONV_DIM
            w_main = w[:, :n_main].astype(BF16)
            w_dt = jnp.concatenate([w[:, n_main:], w[:, n_main:]], axis=1).astype(BF16)
            dtb = jnp.tile(ssd_dt_bias[j].reshape(1, -1), (1, 2))
            alog = jnp.tile(ssd_a_log[j].reshape(1, -1), (1, 2))
            z, xbc, dtl = _ssd_proj(xc, mods, w_main, w_dt, dtb, alog, ssd_conv_w[j],
                                    ssd_conv_b[j].reshape(1, -1), bsz)
            y_f = _ssd_scan(xbc, dtl, 0, ctx_len)
            dskip = jnp.repeat(ssd_d_skip[j], SSD_HEADDIM).reshape(1, -1)
            y = _ssd_scan(xbc, dtl, 1, ctx_len, z=z, y_f=y_f, dskip=dskip,
                          norm_w=ssd_norm_w[j].reshape(1, -1))
            out_w = ssd_out_w[j]
        else:
            q, k, v, g = _ret_proj(xc, mods, ret_in_w[j].astype(BF16), cos, sin, bsz)
            tabs = _ret_tables(ret_decay_logit[j])
            o_f = _ret_scan(q, k, v, tabs[0], 0, ctx_len)
            y = _ret_scan(q, k, v, tabs[1], 1, ctx_len, g=g, o_f=o_f,
                          gn_w=ret_gn_w[j].reshape(1, -1), gn_b=ret_gn_b[j].reshape(1, -1))
            out_w = ret_out_w[j]
        xn, tokens = _out_proj(y, xc, mods, out_w.astype(BF16), ln_mix_g[i].reshape(1, -1),
                               ln_mix_b[i].reshape(1, -1), bsz)
        w_r = jnp.zeros((d, LANES), F32).at[:, :MOE_GROUPS].set(moe_group_w[i])
        w_r = w_r.at[:, MOE_GROUPS:MOE_GROUPS + MOE_EXPERTS].set(moe_expert_w[i])
        b_r = jnp.zeros((1, LANES), F32).at[0, :MOE_GROUPS].set(moe_group_b[i])
        b_r = b_r.at[0, MOE_GROUPS:MOE_GROUPS + MOE_EXPERTS].set(moe_expert_b[i])
        xc, buf = _moe(tokens, xn, mods, w_r, b_r, w_gu, w_d, i, ln_ffn_g[i].reshape(1, -1),
                       ln_ffn_b[i].reshape(1, -1), bsz, buf, ctx_len if i == depth - 1 else 0)
    return xc
```
